```python
import math
import jax, jax.numpy as jnp
from jax import lax
import numpy as np

D_MODEL = 1024
BATCH = 8
SEQ = 2048
DEPTH = 4
DEC_BATCH = 128
DEC_SEQ = 8
PAST_LEN = 16384
PAGE_SIZE = 128

EPS = 1e-6
FFN_DIM = 2816
RET_HEADS = 4
RET_DK = 128
RET_DV = 256
RET_QK_W = RET_HEADS * RET_DK
RET_V_W = RET_HEADS * RET_DV
RET_CHUNK = 128
ROPE_BASE = 10000.0
S5_GROUP = 16
S5_WIDTH = D_MODEL
S5_GROUPS = S5_WIDTH // S5_GROUP
S5_STATE = 64
SSD_INNER = 2 * D_MODEL
SSD_HEADDIM = 64
SSD_HEADS = SSD_INNER // SSD_HEADDIM
SSD_GROUPS = 4
SSD_HPG = SSD_HEADS // SSD_GROUPS
SSD_STATE = 128
SSD_CONV = 4
SSD_CHUNK = 128
SSD_CONV_DIM = SSD_INNER + 2 * SSD_GROUPS * SSD_STATE
N_BRANCH = 3
IN_SPLITS = (RET_QK_W, RET_QK_W, RET_V_W, RET_V_W, S5_WIDTH, SSD_INNER, SSD_CONV_DIM, SSD_HEADS, N_BRANCH * D_MODEL)
IN_DIM = sum(IN_SPLITS)
IN_OFFSETS = tuple(int(o) for o in np.cumsum(IN_SPLITS)[:-1])

kernel_name = "hybrid_retention_s5_ssd_gated_step"


def rmsnorm(x, g):
    xf = x.astype(jnp.float32)
    y = xf * lax.rsqrt(jnp.mean(xf * xf, axis=-1, keepdims=True) + EPS)
    return (y * g.astype(jnp.float32)).astype(x.dtype)


def swiglu_ffn(x, w_gu, w_down):
    g, u = jnp.split(x @ w_gu, 2, axis=-1)
    return (jax.nn.silu(g) * u) @ w_down


def rotary(x, pos):
    half = x.shape[-1] // 2
    inv = ROPE_BASE ** (-jnp.arange(half, dtype=jnp.float32) / half)
    ang = pos.astype(jnp.float32)[:, None] * inv[None, :]
    cos = jnp.cos(ang)[None, :, None, :]
    sin = jnp.sin(ang)[None, :, None, :]
    xf = x.astype(jnp.float32)
    x1, x2 = xf[..., :half], xf[..., half:]
    return jnp.concatenate([x1 * cos - x2 * sin, x2 * cos + x1 * sin], axis=-1)


def retention_chunkwise(q, k, v, s0):
    bsz, L = q.shape[:2]
    C = RET_CHUNK if L % RET_CHUNK == 0 else L
    nc = L // C
    lg = jnp.log1p(-jnp.exp2(-5.0 - jnp.arange(RET_HEADS, dtype=jnp.float32)))
    idx = jnp.arange(C, dtype=jnp.float32)
    diff = idx[:, None] - idx[None, :]
    causal = diff >= 0
    decay = jnp.where(causal[None], jnp.exp(jnp.where(causal, diff, 0.0)[None] * lg[:, None, None]), 0.0)
    q_dec = jnp.exp((idx + 1.0)[:, None] * lg[None, :])[None, :, :, None]
    k_dec = jnp.exp((C - 1.0 - idx)[:, None] * lg[None, :])[None, :, :, None]
    c_dec = jnp.exp(C * lg)[None, :, None, None]

    def chunks(t):
        return jnp.swapaxes(t.reshape((bsz, nc, C) + t.shape[2:]), 0, 1)

    def step(s, inp):
        qc, kc, vc = inp
        vc = vc.astype(jnp.float32)
        scores = jnp.einsum("bihd,bjhd->bhij", qc, kc) * decay[None]
        inner = jnp.einsum("bhij,bjhe->bihe", scores, vc)
        cross = jnp.einsum("bihd,bhde->bihe", qc, s) * q_dec
        s_new = s * c_dec + jnp.einsum("bjhd,bjhe->bhde", kc * k_dec, vc)
        return s_new, inner + cross

    s_fin, o = lax.scan(step, s0.astype(jnp.float32), (chunks(q), chunks(k), chunks(v)))
    return jnp.swapaxes(o, 0, 1).reshape(bsz, L, RET_HEADS, RET_DV), s_fin


def retention_branch(q, k, v, g, pos, s0, ln_g, w_o):
    bsz, L = q.shape[:2]
    q = rotary(q.reshape(bsz, L, RET_HEADS, RET_DK), pos)
    k = rotary(k.reshape(bsz, L, RET_HEADS, RET_DK), pos) * (RET_DK ** -0.5)
    v = v.reshape(bsz, L, RET_HEADS, RET_DV)
    o, s_new = retention_chunkwise(q, k, v, s0)
    mu = jnp.mean(o, axis=-1, keepdims=True)
    var = jnp.mean(jnp.square(o - mu), axis=-1, keepdims=True)
    o = ((o - mu) * lax.rsqrt(var + EPS)).reshape(bsz, L, RET_V_W) * ln_g.astype(jnp.float32)
    o = jax.nn.silu(g.astype(jnp.float32)) * o
    return o.astype(g.dtype) @ w_o, s_new


def s5_branch(u, s0, a_re, a_im, log_dt, b_re, b_im, c_re, c_im, d, w_glu):
    bsz, L = u.shape[:2]
    uf = u.astype(jnp.float32).reshape(bsz, L, S5_GROUPS, S5_GROUP)
    dt = jnp.exp(log_dt.astype(jnp.float32))[:, None]
    ar, ai = a_re.astype(jnp.float32), a_im.astype(jnp.float32)
    mag = jnp.exp(dt * ar)
    abar_re, abar_im = mag * jnp.cos(dt * ai), mag * jnp.sin(dt * ai)
    nr, ni = abar_re - 1.0, abar_im
    den = ar * ar + ai * ai
    f_re, f_im = (nr * ar + ni * ai) / den, (ni * ar - nr * ai) / den
    bb_re = f_re[..., None] * b_re - f_im[..., None] * b_im
    bb_im = f_re[..., None] * b_im + f_im[..., None] * b_re
    bu_re = jnp.einsum("gnc,blgc->blgn", bb_re, uf)
    bu_im = jnp.einsum("gnc,blgc->blgn", bb_im, uf)
    sr, si = s0[..., 0].astype(jnp.float32), s0[..., 1].astype(jnp.float32)
    bu_re = bu_re.at[:, 0].add(abar_re * sr - abar_im * si)
    bu_im = bu_im.at[:, 0].add(abar_re * si + abar_im * sr)
    a_re_t = jnp.broadcast_to(abar_re, bu_re.shape)
    a_im_t = jnp.broadcast_to(abar_im, bu_im.shape)

    def combine(e1, e2):
        a1r, a1i, b1r, b1i = e1
        a2r, a2i, b2r, b2i = e2
        return (a1r * a2r - a1i * a2i, a1r * a2i + a1i * a2r,
                a2r * b1r - a2i * b1i + b2r, a2r * b1i + a2i * b1r + b2i)

    _, _, x_re, x_im = lax.associative_scan(combine, (a_re_t, a_im_t, bu_re, bu_im), axis=1)
    y = jnp.einsum("gcn,blgn->blgc", c_re, x_re) - jnp.einsum("gcn,blgn->blgc", c_im, x_im)
    y = y.reshape(bsz, L, S5_WIDTH) + d.astype(jnp.float32) * uf.reshape(bsz, L, S5_WIDTH)
    y = jax.nn.gelu(y).astype(u.dtype)
    ya, yg = jnp.split(y @ w_glu, 2, axis=-1)
    s_new = jnp.stack([x_re[:, -1], x_im[:, -1]], axis=-1)
    return ya * jax.nn.sigmoid(yg), s_new


def causal_dwconv(xbc, conv_prev, w, b):
    xp = jnp.concatenate([conv_prev.astype(xbc.dtype), xbc], axis=1)
    y = lax.conv_general_dilated(xp, w.astype(xbc.dtype)[:, None, :], window_strides=(1,), padding="VALID",
                                 dimension_numbers=("NWC", "WIO", "NWC"), feature_group_count=SSD_CONV_DIM)
    return y + b.astype(xbc.dtype), xp[:, -(SSD_CONV - 1):]


def ssd_chunked(x, dt, a, bm, cm, h0):
    bsz, L = x.shape[:2]
    C = SSD_CHUNK if L % SSD_CHUNK == 0 else L
    nc = L // C
    causal = jnp.tril(jnp.ones((C, C), dtype=bool))[None, :, :, None, None]

    def chunks(t):
        return jnp.swapaxes(t.reshape((bsz, nc, C) + t.shape[2:]), 0, 1)

    def step(h, inp):
        xc, dtc, bc, cc = inp
        cum = jnp.cumsum(dtc * a, axis=1)
        seg = cum[:, :, None] - cum[:, None, :]
        lmat = jnp.where(causal, jnp.exp(jnp.where(causal, seg, 0.0)), 0.0)
        xdt = xc * dtc[..., None]
        cb = jnp.einsum("bign,bjgn->bijg", cc, bc)
        y_diag = jnp.einsum("bijgh,bjghp->bighp", cb[..., None] * lmat, xdt)
        y_off = jnp.einsum("bign,bghpn->bighp", cc, h) * jnp.exp(cum)[..., None]
        w_end = jnp.exp(cum[:, -1:] - cum)
        h_new = h * jnp.exp(cum[:, -1])[..., None, None] + jnp.einsum("bjgn,bjghp->bghpn", bc, xdt * w_end[..., None])
        return h_new, y_diag + y_off

    h, y = lax.scan(step, h0, (chunks(x), chunks(dt), chunks(bm), chunks(cm)))
    return jnp.swapaxes(y, 0, 1).reshape(x.shape), h


def ssd_branch(z, xbc, dt_raw, h0, conv0, conv_w, conv_b, dt_bias, a_log, d_skip, norm_g, w_o):
    bsz, L = z.shape[:2]
    xbc, conv_new = causal_dwconv(xbc, conv0, conv_w, conv_b)
    xbc = jax.nn.silu(xbc.astype(jnp.float32))
    xs, bm, cm = jnp.split(xbc, [SSD_INNER, SSD_INNER + SSD_GROUPS * SSD_STATE], axis=-1)
    xs = xs.reshape(bsz, L, SSD_GROUPS, SSD_HPG, SSD_HEADDIM)
    bm = bm.reshape(bsz, L, SSD_GROUPS, SSD_STATE)
    cm = cm.reshape(bsz, L, SSD_GROUPS, SSD_STATE)
    dt = jax.nn.softplus(dt_raw.astype(jnp.float32) + dt_bias.astype(jnp.float32)).reshape(bsz, L, SSD_GROUPS, SSD_HPG)
    a = -jnp.exp(a_log.astype(jnp.float32)).reshape(SSD_GROUPS, SSD_HPG)
    h0 = h0.astype(jnp.float32).reshape(bsz, SSD_GROUPS, SSD_HPG, SSD_HEADDIM, SSD_STATE)
    y, h = ssd_chunked(xs, dt, a, bm, cm, h0)
    y = y + d_skip.astype(jnp.float32).reshape(SSD_GROUPS, SSD_HPG)[:, :, None] * xs
    y = (y.reshape(bsz, L, SSD_INNER) * jax.nn.silu(z.astype(jnp.float32))).reshape(bsz, L, SSD_GROUPS, -1)
    y = y * lax.rsqrt(jnp.mean(y * y, axis=-1, keepdims=True) + EPS)
    y = y.reshape(bsz, L, SSD_INNER) * norm_g.astype(jnp.float32)
    return y.astype(z.dtype) @ w_o, h.reshape(bsz, SSD_HEADS, SSD_HEADDIM, SSD_STATE), conv_new


def trunk_layer(x, pos, s_ret, s_s5, s_ssm, s_conv, p):
    bsz, L, _ = x.shape
    x = x + 0.5 * swiglu_ffn(rmsnorm(x, p["ffn1_norm"]), p["ffn1_w_gu"], p["ffn1_w_down"])
    h = rmsnorm(x, p["mix_norm"])
    q, k, v, g_ret, u, z, xbc, dt_raw, gate_logits = jnp.split(h @ p["w_in"], IN_OFFSETS, axis=-1)
    y_ret, s_ret = retention_branch(q, k, v, g_ret, pos, s_ret, p["ret_ln_g"], p["ret_w_o"])
    y_s5, s_s5 = s5_branch(u, s_s5, p["s5_a_re"], p["s5_a_im"], p["s5_log_dt"], p["s5_b_re"], p["s5_b_im"],
                           p["s5_c_re"], p["s5_c_im"], p["s5_d"], p["s5_w_glu"])
    y_ssd, s_ssm, s_conv = ssd_branch(z, xbc, dt_raw, s_ssm, s_conv, p["ssd_conv_w"], p["ssd_conv_b"],
                                      p["ssd_dt_bias"], p["ssd_a_log"], p["ssd_d"], p["ssd_norm"], p["ssd_w_o"])
    gates = jax.nn.sigmoid(gate_logits.astype(jnp.float32)).reshape(bsz, L, N_BRANCH, D_MODEL)
    merged = gates[:, :, 0] * y_ret + gates[:, :, 1] * y_s5 + gates[:, :, 2] * y_ssd
    x = x + merged.astype(x.dtype) @ p["w_out"]
    x = x + 0.5 * swiglu_ffn(rmsnorm(x, p["ffn2_norm"]), p["ffn2_w_gu"], p["ffn2_w_down"])
    return x, s_ret, s_s5, s_ssm, s_conv


def setup_inputs(seed: int = 0) -> dict:
    key = jax.random.key(seed)
    ks = iter(jax.random.split(key, 48))
    f32 = jnp.float32

    def nrm(shape, scale):
        return jax.random.normal(next(ks), shape, f32) * scale

    def gain(shape):
        return 1.0 + 0.01 * jax.random.normal(next(ks), shape, f32)

    n_idx = jnp.arange(S5_STATE, dtype=f32)
    dt0 = jnp.exp(jax.random.uniform(next(ks), (DEPTH, SSD_HEADS), f32, math.log(1e-3), math.log(1e-1)))
    inp = {}
    inp["x_prompt"] = nrm((BATCH, SEQ, D_MODEL), 1.0)
    inp["x_sample"] = nrm((DEC_BATCH, DEC_SEQ, D_MODEL), 1.0)
    inp["state_ret"] = nrm((DEPTH, DEC_BATCH, RET_HEADS, RET_DK, RET_DV), 0.5)
    inp["state_s5"] = nrm((DEPTH, DEC_BATCH, S5_GROUPS, S5_STATE, 2), 0.5)
    inp["state_ssm"] = nrm((DEPTH, DEC_BATCH, SSD_HEADS, SSD_HEADDIM, SSD_STATE), 0.2)
    inp["state_conv"] = nrm((DEPTH, DEC_BATCH, SSD_CONV - 1, SSD_CONV_DIM), 1.0)
    inp["ffn1_norm"] = gain((DEPTH, D_MODEL))
    inp["ffn1_w_gu"] = nrm((DEPTH, D_MODEL, 2 * FFN_DIM), D_MODEL ** -0.5)
    inp["ffn1_w_down"] = nrm((DEPTH, FFN_DIM, D_MODEL), FFN_DIM ** -0.5)
    inp["mix_norm"] = gain((DEPTH, D_MODEL))
    inp["w_in"] = nrm((DEPTH, D_MODEL, IN_DIM), D_MODEL ** -0.5)
    inp["ret_ln_g"] = gain((DEPTH, RET_V_W))
    inp["ret_w_o"] = nrm((DEPTH, RET_V_W, D_MODEL), RET_V_W ** -0.5)
    inp["s5_a_re"] = -0.5 + nrm((DEPTH, S5_GROUPS, S5_STATE), 0.01)
    inp["s5_a_im"] = jnp.pi * n_idx + nrm((DEPTH, S5_GROUPS, S5_STATE), 0.01)
    inp["s5_log_dt"] = jax.random.uniform(next(ks), (DEPTH, S5_GROUPS), f32, math.log(1e-3), math.log(1e-1))
    inp["s5_b_re"] = nrm((DEPTH, S5_GROUPS, S5_STATE, S5_GROUP), (2 * S5_GROUP) ** -0.5)
    inp["s5_b_im"] = nrm((DEPTH, S5_GROUPS, S5_STATE, S5_GROUP), (2 * S5_GROUP) ** -0.5)
    inp["s5_c_re"] = nrm((DEPTH, S5_GROUPS, S5_GROUP, S5_STATE), (2 * S5_STATE) ** -0.5)
    inp["s5_c_im"] = nrm((DEPTH, S5_GROUPS, S5_GROUP, S5_STATE), (2 * S5_STATE) ** -0.5)
    inp["s5_d"] = nrm((DEPTH, S5_WIDTH), 1.0)
    inp["s5_w_glu"] = nrm((DEPTH, S5_WIDTH, 2 * D_MODEL), S5_WIDTH ** -0.5)
    inp["ssd_conv_w"] = nrm((DEPTH, SSD_CONV, SSD_CONV_DIM), SSD_CONV ** -0.5)
    inp["ssd_conv_b"] = nrm((DEPTH, SSD_CONV_DIM), 0.02)
    inp["ssd_dt_bias"] = dt0 + jnp.log(-jnp.expm1(-dt0))
    inp["ssd_a_log"] = jnp.log(jax.random.uniform(next(ks), (DEPTH, SSD_HEADS), f32, 1.0, 16.0))
    inp["ssd_d"] = 1.0 + nrm((DEPTH, SSD_HEADS), 0.1)
    inp["ssd_norm"] = gain((DEPTH, SSD_INNER))
    inp["ssd_w_o"] = nrm((DEPTH, SSD_INNER, D_MODEL), SSD_INNER ** -0.5)
    inp["w_out"] = nrm((DEPTH, D_MODEL, D_MODEL), D_MODEL ** -0.5)
    inp["ffn2_norm"] = gain((DEPTH, D_MODEL))
    inp["ffn2_w_gu"] = nrm((DEPTH, D_MODEL, 2 * FFN_DIM), D_MODEL ** -0.5)
    inp["ffn2_w_down"] = nrm((DEPTH, FFN_DIM, D_MODEL), FFN_DIM ** -0.5)
    inp["final_norm"] = gain((D_MODEL,))
    return inp


def reference(x_prompt, x_sample, state_ret, state_s5, state_ssm, state_conv,
              ffn1_norm, ffn1_w_gu, ffn1_w_down, mix_norm, w_in, ret_ln_g, ret_w_o,
              s5_a_re, s5_a_im, s5_log_dt, s5_b_re, s5_b_im, s5_c_re, s5_c_im, s5_d, s5_w_glu,
              ssd_conv_w, ssd_conv_b, ssd_dt_bias, ssd_a_log, ssd_d, ssd_norm, ssd_w_o,
              w_out, ffn2_norm, ffn2_w_gu, ffn2_w_down, final_norm):
    f32 = jnp.float32
    bp = x_prompt.shape[0]
    pos_p = jnp.arange(x_prompt.shape[1], dtype=jnp.int32)
    pos_s = PAST_LEN + jnp.arange(x_sample.shape[1], dtype=jnp.int32)
    zr = jnp.zeros((bp, RET_HEADS, RET_DK, RET_DV), f32)
    zs5 = jnp.zeros((bp, S5_GROUPS, S5_STATE, 2), f32)
    zssm = jnp.zeros((bp, SSD_HEADS, SSD_HEADDIM, SSD_STATE), f32)
    zconv = jnp.zeros((bp, SSD_CONV - 1, SSD_CONV_DIM), x_prompt.dtype)
    xp, xs = x_prompt, x_sample
    ret_p, ret_s, s5_p, s5_s, ssm_p, ssm_s, conv_p, conv_s = [], [], [], [], [], [], [], []
    for l in range(DEPTH):
        p = dict(ffn1_norm=ffn1_norm[l], ffn1_w_gu=ffn1_w_gu[l], ffn1_w_down=ffn1_w_down[l],
                 mix_norm=mix_norm[l], w_in=w_in[l], ret_ln_g=ret_ln_g[l], ret_w_o=ret_w_o[l],
                 s5_a_re=s5_a_re[l], s5_a_im=s5_a_im[l], s5_log_dt=s5_log_dt[l], s5_b_re=s5_b_re[l],
                 s5_b_im=s5_b_im[l], s5_c_re=s5_c_re[l], s5_c_im=s5_c_im[l], s5_d=s5_d[l], s5_w_glu=s5_w_glu[l],
                 ssd_conv_w=ssd_conv_w[l], ssd_conv_b=ssd_conv_b[l], ssd_dt_bias=ssd_dt_bias[l],
                 ssd_a_log=ssd_a_log[l], ssd_d=ssd_d[l], ssd_norm=ssd_norm[l], ssd_w_o=ssd_w_o[l],
                 w_out=w_out[l], ffn2_norm=ffn2_norm[l], ffn2_w_gu=ffn2_w_gu[l], ffn2_w_down=ffn2_w_down[l])
        xp, r1, s1, m1, c1 = trunk_layer(xp, pos_p, zr, zs5, zssm, zconv, p)
        xs, r2, s2, m2, c2 = trunk_layer(xs, pos_s, state_ret[l], state_s5[l], state_ssm[l], state_conv[l], p)
        ret_p.append(r1); ret_s.append(r2)
        s5_p.append(s1); s5_s.append(s2)
        ssm_p.append(m1); ssm_s.append(m2)
        conv_p.append(c1); conv_s.append(c2)
    y_prompt = rmsnorm(xp, final_norm)
    y_sample = rmsnorm(xs, final_norm)
    return (y_prompt, y_sample, jnp.stack(ret_p), jnp.stack(ret_s), jnp.stack(s5_p), jnp.stack(s5_s),
            jnp.stack(ssm_p), jnp.stack(ssm_s), jnp.stack(conv_p), jnp.stack(conv_s))
```

```python
import functools

import numpy as np
import jax
import jax.numpy as jnp
from jax import lax
from jax.experimental import pallas as pl
from jax.experimental.pallas import tpu as pltpu

F32 = jnp.float32
BF16 = jnp.bfloat16

D_MODEL = 1024
PAST_LEN = 16384
EPS = 1e-6
FFN_DIM = 2816
RET_HEADS = 4
RET_DK = 128
RET_DV = 256
ROPE_BASE = 10000.0
S5_GROUP = 16
S5_GROUPS = 64
S5_STATE = 64
SSD_INNER = 2048
SSD_HEADDIM = 64
SSD_HEADS = 32
SSD_GROUPS = 4
SSD_HPG = 8
SSD_STATE = 128
SSD_CONV = 4
SSD_CONV_DIM = 3072

ROWS = 128
VMEM_LIMIT = 56 * 1024 * 1024

P_Q, P_K, P_V, P_G, P_U, P_Z, P_XBC, P_GATE = 0, 512, 1024, 2048, 3072, 4096, 6144, 9216
P_COLS = 12288


def _cparams(sem):
    return pltpu.CompilerParams(dimension_semantics=sem, vmem_limit_bytes=VMEM_LIMIT)


def _pick(n, cands):
    for c in cands:
        if n % c == 0:
            return c
    raise ValueError(f"no tile for {n}")


def _resident(shape):
    nd = len(shape)
    return pl.BlockSpec(shape, lambda *_: (0,) * nd, pipeline_mode=pl.Buffered(1))


def _rms(x):
    return x * lax.rsqrt(jnp.mean(x * x, axis=-1, keepdims=True) + EPS)


def _silu(x):
    return x * jax.nn.sigmoid(x)


def _softplus(x):
    return jnp.maximum(x, 0.0) + jnp.log1p(jnp.exp(-jnp.abs(x)))


def _dot(a, b):
    return jnp.dot(a, b, preferred_element_type=F32)


def _dot_nt(a, b):
    return lax.dot_general(a, b, (((1,), (1,)), ((), ())), preferred_element_type=F32)


def _dot_f32(a, b):
    return jnp.dot(a, b, preferred_element_type=F32, precision=lax.Precision.HIGHEST)


def _norm_body(x_ref, g_ref, o_ref):
    o_ref[...] = (_rms(x_ref[...]) * g_ref[...]).astype(o_ref.dtype)


def _norm(x, gain, out_dtype):
    t, d = x.shape
    tm = _pick(t, (1024, 512, 256, 128))
    return pl.pallas_call(
        _norm_body,
        grid=(t // tm,),
        in_specs=[pl.BlockSpec((tm, d), lambda i: (i, 0)), pl.BlockSpec((1, d), lambda i: (0, 0))],
        out_specs=pl.BlockSpec((tm, d), lambda i: (i, 0)),
        out_shape=jax.ShapeDtypeStruct((t, d), out_dtype),
        compiler_params=_cparams(("parallel",)),
        name="rmsnorm",
    )(x, gain.reshape(1, d))


def _ffn_body(x_ref, ng_ref, wg_ref, wu_ref, wd_ref, pg_ref, o_ref, h_ref, *, n_chunks, tf):
    x = x_ref[...]
    hb = (_rms(x) * ng_ref[...]).astype(BF16)
    acc = jnp.zeros(x.shape, F32)
    for c in range(n_chunks):
        sl = slice(c * tf, (c + 1) * tf)
        g = _dot(hb, wg_ref[:, sl])
        u = _dot(hb, wu_ref[:, sl])
        acc = acc + _dot((_silu(g) * u).astype(BF16), wd_ref[sl, :])
    y = x + 0.5 * acc
    o_ref[...] = y
    h_ref[...] = (_rms(y) * pg_ref[...]).astype(h_ref.dtype)


def _ffn(x, norm_g, w_g, w_u, w_d, post_g, post_dtype):
    t, d = x.shape
    f = w_g.shape[1]
    tm = _pick(t, (512, 256, 128))
    n_chunks = 2
    tf = f // n_chunks
    return pl.pallas_call(
        functools.partial(_ffn_body, n_chunks=n_chunks, tf=tf),
        grid=(t // tm,),
        in_specs=[
            pl.BlockSpec((tm, d), lambda i: (i, 0)),
            pl.BlockSpec((1, d), lambda i: (0, 0)),
            _resident((d, f)), _resident((d, f)), _resident((f, d)),
            pl.BlockSpec((1, d), lambda i: (0, 0)),
        ],
        out_specs=[pl.BlockSpec((tm, d), lambda i: (i, 0)), pl.BlockSpec((tm, d), lambda i: (i, 0))],
        out_shape=[jax.ShapeDtypeStruct((t, d), F32), jax.ShapeDtypeStruct((t, d), post_dtype)],
        compiler_params=_cparams(("parallel",)),
        name="ffn",
    )(x, norm_g.reshape(1, d), w_g, w_u, w_d, post_g.reshape(1, d))


def _mm_body(a_ref, w_ref, o_ref):
    o_ref[...] = _dot(a_ref[...], w_ref[...])


def _matmul(a, w):
    t, k = a.shape
    n = w.shape[1]
    tm = _pick(t, (1024, 512, 256, 128))
    tn = _pick(n, (1024, 512, 256, 128))
    return pl.pallas_call(
        _mm_body,
        grid=(t // tm, n // tn),
        in_specs=[pl.BlockSpec((tm, k), lambda i, j: (i, 0)), pl.BlockSpec((k, tn), lambda i, j: (0, j))],
        out_specs=pl.BlockSpec((tm, tn), lambda i, j: (i, j)),
        out_shape=jax.ShapeDtypeStruct((t, n), F32),
        compiler_params=_cparams(("parallel", "parallel")),
        name="in_proj",
    )(a, w)


def _merge_body(x_ref, oret_ref, s5_ref, ossd_ref, gl_ref, wro_ref, wglu_ref, wso_ref, wout_ref, o_ref):
    d = D_MODEL
    y_ret = _dot(oret_ref[...], wro_ref[...])
    yag = _dot(s5_ref[...], wglu_ref[...])
    y_s5 = yag[:, :d] * jax.nn.sigmoid(yag[:, d:])
    y_ssd = _dot(ossd_ref[...], wso_ref[...])
    gl = gl_ref[...]
    merged = (jax.nn.sigmoid(gl[:, :d]) * y_ret + jax.nn.sigmoid(gl[:, d:2 * d]) * y_s5
              + jax.nn.sigmoid(gl[:, 2 * d:]) * y_ssd)
    o_ref[...] = x_ref[...] + _dot(merged.astype(BF16), wout_ref[...])


def _merge(x, o_ret, s5_pre, o_ssd, p_main, w_ro, w_glu, w_so, w_out):
    t, d = x.shape
    tm = _pick(t, (512, 256, 128))
    return pl.pallas_call(
        _merge_body,
        grid=(t // tm,),
        in_specs=[
            pl.BlockSpec((tm, d), lambda i: (i, 0)),
            pl.BlockSpec((tm, d), lambda i: (i, 0)),
            pl.BlockSpec((tm, d), lambda i: (i, 0)),
            pl.BlockSpec((tm, 2 * d), lambda i: (i, 0)),
            pl.BlockSpec((tm, 3 * d), lambda i: (i, P_GATE // (3 * d))),
            _resident((d, d)), _resident((d, 2 * d)), _resident((2 * d, d)), _resident((d, d)),
        ],
        out_specs=pl.BlockSpec((tm, d), lambda i: (i, 0)),
        out_shape=jax.ShapeDtypeStruct((t, d), F32),
        compiler_params=_cparams(("parallel",)),
        name="merge",
    )(x, o_ret, s5_pre, o_ssd, p_main, w_ro, w_glu, w_so, w_out)


def _seq_index(nseq):
    r = ROWS // nseq
    i = np.arange(ROWS)
    return i // r, i % r, r


def _block_masks(nseq):
    s, t, _ = _seq_index(nseq)
    same = s[:, None] == s[None, :]
    causal = same & (t[None, :] <= t[:, None])
    return jnp.asarray(causal, F32), jnp.asarray(same, F32)


def _retention_tables(nseq):
    s, t, r = _seq_index(nseq)
    lg = jnp.log1p(-jnp.exp2(-5.0 - jnp.arange(RET_HEADS, dtype=F32)))[:, None, None]
    tf = jnp.asarray(t, F32)
    diff = tf[:, None] - tf[None, :]
    causal = jnp.asarray((s[:, None] == s[None, :]) & (t[None, :] <= t[:, None]))
    decay = jnp.where(causal[None], jnp.exp(jnp.where(causal, diff, 0.0)[None] * lg), 0.0)
    qdec = jnp.broadcast_to(jnp.exp((tf + 1.0)[None, :, None] * lg), (RET_HEADS, ROWS, RET_DV))
    kdec = jnp.broadcast_to(jnp.exp((r - 1.0 - tf)[None, :, None] * lg), (RET_HEADS, ROWS, RET_DK))
    cdec = jnp.broadcast_to(jnp.exp(r * lg), (RET_HEADS, ROWS, RET_DV))
    return decay, qdec, kdec, cdec


def _rotary_tables(pos):
    half = RET_DK // 2
    inv = ROPE_BASE ** (-jnp.arange(half, dtype=F32) / half)
    ang = pos.astype(F32)[:, None] * inv[None, :]
    cos, sin = jnp.cos(ang), jnp.sin(ang)
    return jnp.concatenate([cos, cos], axis=1), jnp.concatenate([-sin, sin], axis=1)


def _ret_body(q_ref, k_ref, v_ref, g_ref, cos_ref, sin_ref, dec_ref, qd_ref, kd_ref, cd_ref, lng_ref,
              s0_ref, o_ref, s_ref, *, nseq):
    r = ROWS // nseq

    @pl.when(pl.program_id(2) == 0)
    def _():
        s_ref[...] = s0_ref[...]

    cos = cos_ref[...]
    sin = sin_ref[...]

    def rot(x):
        return x * cos + pltpu.roll(x, RET_DK // 2, 1) * sin

    q = rot(q_ref[...])
    k = rot(k_ref[...]) * (RET_DK ** -0.5)
    qb = q.astype(BF16)
    vb = v_ref[...].astype(BF16)
    scores = _dot_nt(qb, k.astype(BF16)) * dec_ref[...]
    inner = _dot(scores.astype(BF16), vb)
    kdt = (k * kd_ref[...]).T
    cd = cd_ref[...]
    if nseq == 1:
        s = s_ref[0]
        cross = _dot(qb, s.astype(BF16))
        s_ref[0] = s * cd + _dot(kdt.astype(BF16), vb)
    else:
        sh = r.bit_length() - 1
        row_seq = lax.broadcasted_iota(jnp.int32, (ROWS, 1), 0) >> sh
        lane_seq = lax.broadcasted_iota(jnp.int32, (1, ROWS), 1) >> sh

        def seq_step(i, cross):
            s = s_ref[i]
            cross = jnp.where(row_seq == i, _dot(qb, s.astype(BF16)), cross)
            s_ref[i] = s * cd + _dot(jnp.where(lane_seq == i, kdt, 0.0).astype(BF16), vb)
            return cross

        cross = lax.fori_loop(0, nseq, seq_step, jnp.zeros((ROWS, RET_DV), F32))
    o = inner + cross * qd_ref[...]
    mu = jnp.mean(o, axis=-1, keepdims=True)
    oc = o - mu
    var = jnp.mean(oc * oc, axis=-1, keepdims=True)
    on = (oc * lax.rsqrt(var + EPS)) * lng_ref[...]
    o_ref[...] = (_silu(g_ref[...]) * on).astype(o_ref.dtype)


def _retention(p_main, row0, nblk, nc, nseq, cos, sin, tabs, ln_g, s0):
    h = RET_HEADS
    decay, qdec, kdec, cdec = tabs

    def rmap(b, c):
        return row0 + b * nc + c

    in_specs = [
        pl.BlockSpec((ROWS, RET_DK), lambda b, hh, c: (rmap(b, c), P_Q // RET_DK + hh)),
        pl.BlockSpec((ROWS, RET_DK), lambda b, hh, c: (rmap(b, c), P_K // RET_DK + hh)),
        pl.BlockSpec((ROWS, RET_DV), lambda b, hh, c: (rmap(b, c), P_V // RET_DV + hh)),
        pl.BlockSpec((ROWS, RET_DV), lambda b, hh, c: (rmap(b, c), P_G // RET_DV + hh)),
        pl.BlockSpec((ROWS, RET_DK), lambda b, hh, c: (c, 0)),
        pl.BlockSpec((ROWS, RET_DK), lambda b, hh, c: (c, 0)),
        pl.BlockSpec((None, ROWS, ROWS), lambda b, hh, c: (hh, 0, 0)),
        pl.BlockSpec((None, ROWS, RET_DV), lambda b, hh, c: (hh, 0, 0)),
        pl.BlockSpec((None, ROWS, RET_DK), lambda b, hh, c: (hh, 0, 0)),
        pl.BlockSpec((None, ROWS, RET_DV), lambda b, hh, c: (hh, 0, 0)),
        pl.BlockSpec((1, RET_DV), lambda b, hh, c: (0, hh)),
        pl.BlockSpec((nseq, None, RET_DK, RET_DV), lambda b, hh, c: (b, hh, 0, 0)),
    ]
    out_specs = [
        pl.BlockSpec((ROWS, RET_DV), lambda b, hh, c: (b * nc + c, hh)),
        pl.BlockSpec((nseq, None, RET_DK, RET_DV), lambda b, hh, c: (b, hh, 0, 0)),
    ]
    out_shape = [
        jax.ShapeDtypeStruct((nblk * nc * ROWS, h * RET_DV), BF16),
        jax.ShapeDtypeStruct((nblk * nseq, h, RET_DK, RET_DV), F32),
    ]
    return pl.pallas_call(
        functools.partial(_ret_body, nseq=nseq),
        grid=(nblk, h, nc),
        in_specs=in_specs, out_specs=out_specs, out_shape=out_shape,
        compiler_params=_cparams(("parallel", "parallel", "arbitrary")),
        name="retention",
    )(p_main, p_main, p_main, p_main, cos, sin, decay, qdec, kdec, cdec, ln_g.reshape(1, -1), s0)


def _s5_disc_body(ar_ref, ai_ref, ldt_ref, br_ref, bi_ref, abr_ref, abi_ref, bbr_ref, bbi_ref):
    ar = ar_ref[...]
    ai = ai_ref[...]
    dt = jnp.exp(ldt_ref[...])
    mag = jnp.exp(dt * ar)
    abr = mag * jnp.cos(dt * ai)
    abi = mag * jnp.sin(dt * ai)
    nr = abr - 1.0
    ni = abi
    den = ar * ar + ai * ai
    f_re = (nr * ar + ni * ai) / den
    f_im = (ni * ar - nr * ai) / den
    br = br_ref[...]
    bi = bi_ref[...]
    abr_ref[...] = abr
    abi_ref[...] = abi
    bbr_ref[...] = f_re * br - f_im * bi
    bbi_ref[...] = f_re * bi + f_im * br


def _s5_discretize(a_re, a_im, log_dt, b_re, b_im):
    g, n, c = b_re.shape
    flat = (g * c * n // 128, 128)

    def rep(a):
        return jnp.broadcast_to(a[:, None, :], (g, c, n)).reshape(flat)

    args = (rep(a_re), rep(a_im), rep(jnp.broadcast_to(log_dt[:, None], (g, n))),
            jnp.swapaxes(b_re, 1, 2).reshape(flat), jnp.swapaxes(b_im, 1, 2).reshape(flat))
    spec = pl.BlockSpec(flat, lambda: (0, 0))
    abr, abi, bbr, bbi = pl.pallas_call(
        _s5_disc_body,
        in_specs=[spec] * 5, out_specs=[spec] * 4,
        out_shape=[jax.ShapeDtypeStruct(flat, F32)] * 4,
        name="s5_discretize",
    )(*args)
    abr = abr.reshape(g, c, n)[:, 0, :].reshape(1, g * n)
    abi = abi.reshape(g, c, n)[:, 0, :].reshape(1, g * n)
    return abr, abi, bbr.reshape(g, c, n), bbi.reshape(g, c, n)


def _s5_block_weights(bb_re, bb_im, c_re, c_im):
    g, c, n = bb_re.shape
    ng = 16
    eye = jnp.eye(ng, dtype=F32)

    def b_blk(bb):
        x = bb.reshape(g // ng, ng, c, n)
        return jnp.einsum("kgcn,gh->kgchn", x, eye).reshape(g // ng, ng * c, ng * n).astype(BF16)

    def c_blk(cc):
        x = cc.reshape(g // ng, ng, c, n)
        return jnp.einsum("kgcn,gh->kgnhc", x, eye).reshape(g // ng, ng * n, ng * c).astype(BF16)

    return b_blk(bb_re), b_blk(bb_im), c_blk(c_re), c_blk(c_im)


def _s5_body(u_ref, bre_ref, bim_ref, cre_ref, cim_ref, ar_ref, ai_ref, d_ref, s0r_ref, s0i_ref,
             y_ref, sr_ref, si_ref, bur, bui, *, tl, bb):
    rows = tl * bb
    n_cg = bre_ref.shape[0]
    cw = bre_ref.shape[1]
    sw = bre_ref.shape[2]
    lc = 512

    @pl.when(pl.program_id(1) == 0)
    def _():
        sr_ref[...] = s0r_ref[...]
        si_ref[...] = s0i_ref[...]

    u = u_ref[...].reshape(rows, n_cg * cw)
    ub = u.astype(BF16)
    for cg in range(n_cg):
        ucg = ub[:, cg * cw:(cg + 1) * cw]
        bur[:, cg * sw:(cg + 1) * sw] = _dot(ucg, bre_ref[cg])
        bui[:, cg * sw:(cg + 1) * sw] = _dot(ucg, bim_ref[cg])

    for bt in range(bb // 8):
        rs = slice(bt * 8, (bt + 1) * 8)
        for ci in range(n_cg * sw // lc):
            cs = slice(ci * lc, (ci + 1) * lc)
            ar = jnp.broadcast_to(ar_ref[:, cs], (8, lc))
            ai = jnp.broadcast_to(ai_ref[:, cs], (8, lc))

            def step(l, carry, cs=cs, ar=ar, ai=ai, bt=bt):
                xr, xi = carry
                r0 = pl.multiple_of(l * bb + bt * 8, 8)
                nr = (ar * xr - ai * xi) + bur[pl.ds(r0, 8), cs]
                ni = (ar * xi + ai * xr) + bui[pl.ds(r0, 8), cs]
                bur[pl.ds(r0, 8), cs] = nr
                bui[pl.ds(r0, 8), cs] = ni
                return nr, ni

            xr, xi = lax.fori_loop(0, tl, step, (sr_ref[rs, cs], si_ref[rs, cs]))
            sr_ref[rs, cs] = xr
            si_ref[rs, cs] = xi

    for cg in range(n_cg):
        xs = slice(cg * sw, (cg + 1) * sw)
        us = slice(cg * cw, (cg + 1) * cw)
        y = _dot(bur[:, xs].astype(BF16), cre_ref[cg]) - _dot(bui[:, xs].astype(BF16), cim_ref[cg])
        y = y + d_ref[:, us] * u[:, us]
        y_ref[:, :, us] = jax.nn.gelu(y).reshape(tl, bb, cw)


def _s5(u_tm, wts, abr, abi, d, s0r, s0i, tl, bb):
    l, b, w = u_tm.shape
    bre, bim, cre, cim = wts
    ns = abr.shape[1]
    rows = tl * bb
    return pl.pallas_call(
        functools.partial(_s5_body, tl=tl, bb=bb),
        grid=(b // bb, l // tl),
        in_specs=[
            pl.BlockSpec((tl, bb, w), lambda i, j: (j, i, 0)),
            _resident(bre.shape), _resident(bim.shape), _resident(cre.shape), _resident(cim.shape),
            _resident((1, ns)), _resident((1, ns)), _resident((1, w)),
            pl.BlockSpec((bb, ns), lambda i, j: (i, 0)),
            pl.BlockSpec((bb, ns), lambda i, j: (i, 0)),
        ],
        out_specs=[
            pl.BlockSpec((tl, bb, w), lambda i, j: (j, i, 0)),
            pl.BlockSpec((bb, ns), lambda i, j: (i, 0)),
            pl.BlockSpec((bb, ns), lambda i, j: (i, 0)),
        ],
        out_shape=[
            jax.ShapeDtypeStruct((l, b, w), F32),
            jax.ShapeDtypeStruct((b, ns), F32),
            jax.ShapeDtypeStruct((b, ns), F32),
        ],
        scratch_shapes=[pltpu.VMEM((rows, ns), F32), pltpu.VMEM((rows, ns), F32)],
        compiler_params=_cparams(("parallel", "arbitrary")),
        name="s5",
    )(u_tm, bre, bim, cre, cim, abr, abi, d.reshape(1, w), s0r, s0i)


def _ssd_body(z_ref, xs_ref, bm_ref, cm_ref, dt_ref, px_ref, pb_ref, pc_ref, wx_ref, wb_ref, wc_ref,
              bx_ref, bb_ref, bc_ref, dtb_ref, alog_ref, dsk_ref, ng_ref, tri_ref, seqm_ref, h0_ref,
              o_ref, h_ref, tx, tb, tc, *, nseq):
    r = ROWS // nseq
    pd = SSD_HEADDIM
    row = lax.broadcasted_iota(jnp.int32, (ROWS, 1), 0)
    t_idx = row & (r - 1)
    lane = lax.broadcasted_iota(jnp.int32, (1, ROWS), 1)
    lo = lane < pd

    @pl.when(pl.program_id(2) == 0)
    def _():
        h_ref[...] = h0_ref[...]
        if nseq == 1:
            tx[...] = px_ref[0]
            tb[...] = pb_ref[0]
            tc[...] = pc_ref[0]

    def conv(x_ref, tail, prev_ref, w_ref, b_ref):
        x = x_ref[...]
        w = x.shape[1]
        if nseq == 1:
            z8 = jnp.concatenate([tail[...], jnp.zeros((ROWS - 8, w), F32)], axis=0)
        else:
            z8 = prev_ref[...].reshape(ROWS, w)
        acc = b_ref[...] + w_ref[SSD_CONV - 1:SSD_CONV, :] * x
        for k in range(1, SSD_CONV):
            xk = jnp.where(t_idx >= k, pltpu.roll(x, k, 0), pltpu.roll(z8, ROWS - 8 + k, 0))
            acc = acc + w_ref[SSD_CONV - 1 - k:SSD_CONV - k, :] * xk
        if nseq == 1:
            tail[...] = x[ROWS - 8:, :]
        return _silu(acc)

    xs = conv(xs_ref, tx, px_ref, wx_ref, bx_ref)
    bm = conv(bm_ref, tb, pb_ref, wb_ref, bb_ref)
    cm = conv(cm_ref, tc, pc_ref, wc_ref, bc_ref)
    bmb = bm.astype(BF16)
    cmb = cm.astype(BF16)

    dt = _softplus(dt_ref[...] + dtb_ref[...])
    dta = dt * (-jnp.exp(alog_ref[...]))
    tri = tri_ref[...]
    cum = _dot_f32(tri, dta)
    clast = _dot_f32(seqm_ref[...], dta)
    cum_t = cum.T
    e = jnp.exp(cum)
    wend = jnp.exp(clast - cum)
    ecl = jnp.exp(clast)
    causal = tri > 0.0
    cb = _dot_nt(cmb, bmb)

    def col(a, h):
        return a[:, h:h + 1]

    def pair(a, h0):
        return jnp.where(lo, col(a, h0), col(a, h0 + 1))

    ys = []
    for p in range(SSD_HPG // 2):
        h0 = 2 * p
        sl = slice(p * 2 * pd, (p + 1) * 2 * pd)
        xs_p = xs[:, sl]
        xdt = xs_p * pair(dt, h0)

        def lmat(h):
            seg = col(cum, h) - cum_t[h:h + 1, :]
            return jnp.where(causal, jnp.exp(jnp.where(causal, seg, 0.0)), 0.0)

        m0 = (cb * lmat(h0)).astype(BF16)
        m1 = (cb * lmat(h0 + 1)).astype(BF16)
        y = _dot(m0, jnp.where(lo, xdt, 0.0).astype(BF16)) + _dot(m1, jnp.where(lo, 0.0, xdt).astype(BF16))
        xdtw_t = (xdt * pair(wend, h0)).T
        ecl_t = pair(ecl, h0).T
        if nseq == 1:
            hp = h_ref[0, h0:h0 + 2].reshape(2 * pd, SSD_STATE)
            yoff = _dot_nt(cmb, hp.astype(BF16))
            hn = hp * ecl_t[:, 0:1] + _dot(xdtw_t.astype(BF16), bmb)
            h_ref[0, h0:h0 + 2] = hn.reshape(2, pd, SSD_STATE)
        else:
            row_seq = row >> (r.bit_length() - 1)
            lane_seq = lane >> (r.bit_length() - 1)

            def seq_step(i, yoff, h0=h0, xdtw_t=xdtw_t, ecl_t=ecl_t):
                hp = h_ref[i, h0:h0 + 2].reshape(2 * pd, SSD_STATE)
                yoff = jnp.where(row_seq == i, _dot_nt(cmb, hp.astype(BF16)), yoff)
                fac = jnp.sum(jnp.where(lane == i * r, ecl_t, 0.0), axis=1, keepdims=True)
                hn = hp * fac + _dot(jnp.where(lane_seq == i, xdtw_t, 0.0).astype(BF16), bmb)
                h_ref[i, h0:h0 + 2] = hn.reshape(2, pd, SSD_STATE)
                return yoff

            yoff = lax.fori_loop(0, nseq, seq_step, jnp.zeros((ROWS, 2 * pd), F32))
        ys.append(y + yoff * pair(e, h0) + dsk_ref[:, sl] * xs_p)
    y = jnp.concatenate(ys, axis=1) * _silu(z_ref[...])
    y = _rms(y) * ng_ref[...]
    o_ref[...] = y.astype(o_ref.dtype)


def _ssd(p_main, p_dt, row0, nblk, nc, nseq, conv0, conv_w, conv_b, dtb, alog, dskip, norm_g, h0):
    g = SSD_GROUPS
    gw = SSD_INNER // g
    ns = SSD_STATE
    tri, seqm = _block_masks(nseq)
    xo = P_XBC
    bo = P_XBC + SSD_INNER
    co = bo + g * ns

    def rmap(b, c):
        return row0 + b * nc + c

    def cols(width, off):
        return lambda b, gg, c: (0, off // width + gg)

    in_specs = [
        pl.BlockSpec((ROWS, gw), lambda b, gg, c: (rmap(b, c), P_Z // gw + gg)),
        pl.BlockSpec((ROWS, gw), lambda b, gg, c: (rmap(b, c), xo // gw + gg)),
        pl.BlockSpec((ROWS, ns), lambda b, gg, c: (rmap(b, c), bo // ns + gg)),
        pl.BlockSpec((ROWS, ns), lambda b, gg, c: (rmap(b, c), co // ns + gg)),
        pl.BlockSpec((ROWS, 128), lambda b, gg, c: (rmap(b, c), gg)),
        pl.BlockSpec((nseq, 8, gw), lambda b, gg, c: (b, 0, gg)),
        pl.BlockSpec((nseq, 8, ns), lambda b, gg, c: (b, 0, SSD_INNER // ns + gg)),
        pl.BlockSpec((nseq, 8, ns), lambda b, gg, c: (b, 0, SSD_INNER // ns + g + gg)),
        pl.BlockSpec((SSD_CONV, gw), cols(gw, 0)),
        pl.BlockSpec((SSD_CONV, ns), cols(ns, SSD_INNER)),
        pl.BlockSpec((SSD_CONV, ns), cols(ns, SSD_INNER + g * ns)),
        pl.BlockSpec((1, gw), cols(gw, 0)),
        pl.BlockSpec((1, ns), cols(ns, SSD_INNER)),
        pl.BlockSpec((1, ns), cols(ns, SSD_INNER + g * ns)),
        pl.BlockSpec((1, 128), cols(128, 0)),
        pl.BlockSpec((1, 128), cols(128, 0)),
        pl.BlockSpec((1, gw), cols(gw, 0)),
        pl.BlockSpec((1, gw), cols(gw, 0)),
        pl.BlockSpec((ROWS, ROWS), lambda b, gg, c: (0, 0)),
        pl.BlockSpec((ROWS, ROWS), lambda b, gg, c: (0, 0)),
        pl.BlockSpec((nseq, SSD_HPG, SSD_HEADDIM, ns), lambda b, gg, c: (b, gg, 0, 0)),
    ]
    out_specs = [
        pl.BlockSpec((ROWS, gw), lambda b, gg, c: (b * nc + c, gg)),
        pl.BlockSpec((nseq, SSD_HPG, SSD_HEADDIM, ns), lambda b, gg, c: (b, gg, 0, 0)),
    ]
    out_shape = [
        jax.ShapeDtypeStruct((nblk * nc * ROWS, SSD_INNER), BF16),
        jax.ShapeDtypeStruct((nblk * nseq, SSD_HEADS, SSD_HEADDIM, ns), F32),
    ]
    return pl.pallas_call(
        functools.partial(_ssd_body, nseq=nseq),
        grid=(nblk, g, nc),
        in_specs=in_specs, out_specs=out_specs, out_shape=out_shape,
        scratch_shapes=[pltpu.VMEM((8, gw), F32), pltpu.VMEM((8, ns), F32), pltpu.VMEM((8, ns), F32)],
        compiler_params=_cparams(("parallel", "parallel", "arbitrary")),
        name="ssd",
    )(p_main, p_main, p_main, p_main, p_dt, conv0, conv0, conv0, conv_w, conv_w, conv_w,
      conv_b, conv_b, conv_b, dtb, alog, dskip, norm_g, tri, seqm, h0)


def _group_pad(v):
    return jnp.pad(v.reshape(SSD_GROUPS, SSD_HPG), ((0, 0), (0, 128 - SSD_HPG))).reshape(1, -1)


def _prep_w_in(w_in):
    o_dt = P_XBC + SSD_CONV_DIM
    w_main = jnp.concatenate([w_in[:, :o_dt], w_in[:, o_dt + SSD_HEADS:]], axis=1).astype(BF16)
    w_dt = w_in[:, o_dt:o_dt + SSD_HEADS].reshape(-1, SSD_GROUPS, SSD_HPG)
    w_dt = jnp.pad(w_dt, ((0, 0), (0, 0), (0, 128 - SSD_HPG))).reshape(-1, SSD_GROUPS * 128).astype(BF16)
    return w_main, w_dt


def _pad_conv(c):
    return jnp.pad(c, ((0, 0), (8 - (SSD_CONV - 1), 0), (0, 0)))


def kernel(x_prompt, x_sample, state_ret, state_s5, state_ssm, state_conv, ffn1_norm, ffn1_w_gu, ffn1_w_down, mix_norm, w_in, ret_ln_g, ret_w_o, s5_a_re, s5_a_im, s5_log_dt, s5_b_re, s5_b_im, s5_c_re, s5_c_im, s5_d, s5_w_glu, ssd_conv_w, ssd_conv_b, ssd_dt_bias, ssd_a_log, ssd_d, ssd_norm, ssd_w_o, w_out, ffn2_norm, ffn2_w_gu, ffn2_w_down, final_norm):
    depth = w_in.shape[0]
    bp, lp, d = x_prompt.shape
    bs, ls, _ = x_sample.shape
    tp, ts = bp * lp, bs * ls
    nseq_s = ROWS // ls
    assert lp % ROWS == 0 and ROWS % ls == 0 and bs % nseq_s == 0 and tp % ROWS == 0
    ncp = lp // ROWS
    nblk_s = bs // nseq_s

    x = jnp.concatenate([x_prompt.reshape(tp, d), x_sample.reshape(ts, d)], axis=0)

    cos_p, sin_p = _rotary_tables(jnp.arange(lp, dtype=jnp.int32))
    cos_s, sin_s = _rotary_tables(PAST_LEN + jnp.arange(ROWS, dtype=jnp.int32) % ls)
    rtab_p = _retention_tables(1)
    rtab_s = _retention_tables(nseq_s)
    zero_ret = jnp.zeros((bp, RET_HEADS, RET_DK, RET_DV), F32)
    zero_s5 = jnp.zeros((bp, S5_GROUPS * S5_STATE), F32)
    zero_ssm = jnp.zeros((bp, SSD_HEADS, SSD_HEADDIM, SSD_STATE), F32)
    zero_conv = jnp.zeros((bp, 8, SSD_CONV_DIM), F32)

    outs = [[] for _ in range(8)]
    h_mix = None
    for l in range(depth):
        f = ffn1_w_gu.shape[2] // 2
        x, h_mix = _ffn(x, ffn1_norm[l], ffn1_w_gu[l, :, :f].astype(BF16), ffn1_w_gu[l, :, f:].astype(BF16),
                        ffn1_w_down[l].astype(BF16), mix_norm[l], BF16)
        w_main, w_dt = _prep_w_in(w_in[l])
        p_main = _matmul(h_mix, w_main)
        p_dt = _matmul(h_mix, w_dt)

        o_ret_p, ret_p = _retention(p_main, 0, bp, ncp, 1, cos_p, sin_p, rtab_p, ret_ln_g[l], zero_ret)
        o_ret_s, ret_s = _retention(p_main, tp // ROWS, nblk_s, 1, nseq_s, cos_s, sin_s, rtab_s,
                                    ret_ln_g[l], state_ret[l])

        abr, abi, bbr, bbi = _s5_discretize(s5_a_re[l], s5_a_im[l], s5_log_dt[l], s5_b_re[l], s5_b_im[l])
        wts = _s5_block_weights(bbr, bbi, s5_c_re[l], s5_c_im[l])
        u_p = jnp.swapaxes(p_main[:tp, P_U:P_U + d].reshape(bp, lp, d), 0, 1)
        u_s = jnp.swapaxes(p_main[tp:, P_U:P_U + d].reshape(bs, ls, d), 0, 1)
        y5_p, s5r_p, s5i_p = _s5(u_p, wts, abr, abi, s5_d[l], zero_s5, zero_s5, tl=32, bb=bp)
        ss = state_s5[l].reshape(bs, S5_GROUPS * S5_STATE, 2)
        y5_s, s5r_s, s5i_s = _s5(u_s, wts, abr, abi, s5_d[l], ss[..., 0], ss[..., 1], tl=ls, bb=32)
        s5_pre = jnp.concatenate([jnp.swapaxes(y5_p, 0, 1).reshape(tp, d),
                                  jnp.swapaxes(y5_s, 0, 1).reshape(ts, d)], axis=0).astype(BF16)

        dtb = _group_pad(ssd_dt_bias[l])
        alog = _group_pad(ssd_a_log[l])
        dskip = jnp.repeat(ssd_d[l], SSD_HEADDIM).reshape(1, SSD_INNER)
        ng = ssd_norm[l].reshape(1, SSD_INNER)
        cw, cbias = ssd_conv_w[l], ssd_conv_b[l].reshape(1, -1)
        o_ssd_p, ssm_p = _ssd(p_main, p_dt, 0, bp, ncp, 1, zero_conv, cw, cbias, dtb, alog, dskip, ng, zero_ssm)
        o_ssd_s, ssm_s = _ssd(p_main, p_dt, tp // ROWS, nblk_s, 1, nseq_s, _pad_conv(state_conv[l]), cw, cbias,
                              dtb, alog, dskip, ng, state_ssm[l])
        xbc = p_main[:, P_XBC:P_XBC + SSD_CONV_DIM]
        conv_p = xbc[:tp].reshape(bp, lp, -1)[:, lp - (SSD_CONV - 1):]
        conv_s = xbc[tp:].reshape(bs, ls, -1)[:, ls - (SSD_CONV - 1):]

        x = _merge(x, jnp.concatenate([o_ret_p, o_ret_s], axis=0), s5_pre,
                   jnp.concatenate([o_ssd_p, o_ssd_s], axis=0), p_main,
                   ret_w_o[l].astype(BF16), s5_w_glu[l].astype(BF16), ssd_w_o[l].astype(BF16),
                   w_out[l].astype(BF16))
        last = l == depth - 1
        x, y_fin = _ffn(x, ffn2_norm[l], ffn2_w_gu[l, :, :f].astype(BF16), ffn2_w_gu[l, :, f:].astype(BF16),
                        ffn2_w_down[l].astype(BF16), final_norm if last else mix_norm[l], F32 if last else BF16)

        def s5_state(sr, si, b):
            return jnp.stack([sr, si], axis=-1).reshape(b, S5_GROUPS, S5_STATE, 2)

        for lst, v in zip(outs, (ret_p, ret_s, s5_state(s5r_p, s5i_p, bp), s5_state(s5r_s, s5i_s, bs),
                                 ssm_p, ssm_s, conv_p, conv_s)):
            lst.append(v)

    y_prompt = y_fin[:tp].reshape(bp, lp, d)
    y_sample = y_fin[tp:].reshape(bs, ls, d)
    return (y_prompt, y_sample) + tuple(jnp.stack(o) for o in outs)
```

```python
import functools

import numpy as np
import jax
import jax.numpy as jnp
from jax import lax
from jax.experimental import pallas as pl
from jax.experimental.pallas import tpu as pltpu

F32 = jnp.float32
BF16 = jnp.bfloat16

D_MODEL = 1024
PAST_LEN = 16384
EPS = 1e-6
RET_HEADS = 4
RET_DK = 128
RET_DV = 256
ROPE_BASE = 10000.0
S5_GROUPS = 64
S5_STATE = 64
SSD_INNER = 2048
SSD_HEADDIM = 64
SSD_HEADS = 32
SSD_GROUPS = 4
SSD_HPG = 8
SSD_STATE = 128
SSD_CONV = 4
SSD_CONV_DIM = 3072

ROWS = 128
VMEM_LIMIT = 56 * 1024 * 1024

P_Q, P_K, P_V, P_G, P_U, P_Z, P_XBC, P_GATE = 0, 512, 1024, 2048, 3072, 4096, 6144, 9216


def _cparams(sem):
    return pltpu.CompilerParams(dimension_semantics=sem, vmem_limit_bytes=VMEM_LIMIT)


def _pick(n, cands):
    for c in cands:
        if n % c == 0:
            return c
    raise ValueError(f"no tile for {n}")


def _resident(shape):
    nd = len(shape)
    return pl.BlockSpec(shape, lambda *_: (0,) * nd, pipeline_mode=pl.Buffered(1))


def _rms(x):
    return x * lax.rsqrt(jnp.mean(x * x, axis=-1, keepdims=True) + EPS)


def _silu(x):
    return x * jax.nn.sigmoid(x)


def _softplus(x):
    return jnp.maximum(x, 0.0) + jnp.log1p(jnp.exp(-jnp.abs(x)))


def _dot(a, b):
    return jnp.dot(a, b, preferred_element_type=F32)


def _dot_nt(a, b):
    return lax.dot_general(a, b, (((1,), (1,)), ((), ())), preferred_element_type=F32)


def _dot_tn(a, b):
    return lax.dot_general(a, b, (((0,), (0,)), ((), ())), preferred_element_type=F32)


def _dot_f32(a, b):
    return jnp.dot(a, b, preferred_element_type=F32, precision=lax.Precision.HIGHEST)


def _ffn_body(x_ref, ng_ref, wg_ref, wu_ref, wd_ref, pg_ref, o_ref, h_ref, *, n_chunks, tf):
    x = x_ref[...]
    hb = (_rms(x) * ng_ref[...]).astype(BF16)
    acc = jnp.zeros(x.shape, F32)
    for c in range(n_chunks):
        sl = slice(c * tf, (c + 1) * tf)
        g = _dot(hb, wg_ref[:, sl])
        u = _dot(hb, wu_ref[:, sl])
        acc = acc + _dot((_silu(g) * u).astype(BF16), wd_ref[sl, :])
    y = x + 0.5 * acc
    o_ref[...] = y
    h_ref[...] = (_rms(y) * pg_ref[...]).astype(h_ref.dtype)


def _ffn(x, norm_g, w_g, w_u, w_d, post_g, post_dtype):
    t, d = x.shape
    f = w_g.shape[1]
    tm = _pick(t, (512, 256, 128))
    n_chunks = 2
    tf = f // n_chunks
    return pl.pallas_call(
        functools.partial(_ffn_body, n_chunks=n_chunks, tf=tf),
        grid=(t // tm,),
        in_specs=[
            pl.BlockSpec((tm, d), lambda i: (i, 0)),
            pl.BlockSpec((1, d), lambda i: (0, 0)),
            _resident((d, f)), _resident((d, f)), _resident((f, d)),
            pl.BlockSpec((1, d), lambda i: (0, 0)),
        ],
        out_specs=[pl.BlockSpec((tm, d), lambda i: (i, 0)), pl.BlockSpec((tm, d), lambda i: (i, 0))],
        out_shape=[jax.ShapeDtypeStruct((t, d), F32), jax.ShapeDtypeStruct((t, d), post_dtype)],
        compiler_params=_cparams(("parallel",)),
        name="ffn",
    )(x, norm_g.reshape(1, d), w_g, w_u, w_d, post_g.reshape(1, d))


def _mm_body(a_ref, w_ref, o_ref):
    o_ref[...] = _dot(a_ref[...], w_ref[...])


def _matmul(a, w):
    t, k = a.shape
    n = w.shape[1]
    tm = _pick(t, (1024, 512, 256, 128))
    tn = _pick(n, (1024, 512, 256, 128))
    return pl.pallas_call(
        _mm_body,
        grid=(t // tm, n // tn),
        in_specs=[pl.BlockSpec((tm, k), lambda i, j: (i, 0)), pl.BlockSpec((k, tn), lambda i, j: (0, j))],
        out_specs=pl.BlockSpec((tm, tn), lambda i, j: (i, j)),
        out_shape=jax.ShapeDtypeStruct((t, n), F32),
        compiler_params=_cparams(("parallel", "parallel")),
        name="in_proj",
    )(a, w)


def _merge_body(x_ref, oret_ref, s5_ref, ossd_ref, gl_ref, wro_ref, wglu_ref, wso_ref, wout_ref, o_ref):
    d = D_MODEL
    y_ret = _dot(oret_ref[...], wro_ref[...])
    yag = _dot(s5_ref[...], wglu_ref[...])
    y_s5 = yag[:, :d] * jax.nn.sigmoid(yag[:, d:])
    y_ssd = _dot(ossd_ref[...], wso_ref[...])
    gl = gl_ref[...]
    merged = (jax.nn.sigmoid(gl[:, :d]) * y_ret + jax.nn.sigmoid(gl[:, d:2 * d]) * y_s5
              + jax.nn.sigmoid(gl[:, 2 * d:]) * y_ssd)
    o_ref[...] = x_ref[...] + _dot(merged.astype(BF16), wout_ref[...])


def _merge(x, o_ret, s5_pre, o_ssd, p_main, w_ro, w_glu, w_so, w_out):
    t, d = x.shape
    tm = _pick(t, (512, 256, 128))
    return pl.pallas_call(
        _merge_body,
        grid=(t // tm,),
        in_specs=[
            pl.BlockSpec((tm, d), lambda i: (i, 0)),
            pl.BlockSpec((tm, d), lambda i: (i, 0)),
            pl.BlockSpec((tm, d), lambda i: (i, 0)),
            pl.BlockSpec((tm, 2 * d), lambda i: (i, 0)),
            pl.BlockSpec((tm, 3 * d), lambda i: (i, P_GATE // (3 * d))),
            _resident((d, d)), _resident((d, 2 * d)), _resident((2 * d, d)), _resident((d, d)),
        ],
        out_specs=pl.BlockSpec((tm, d), lambda i: (i, 0)),
        out_shape=jax.ShapeDtypeStruct((t, d), F32),
        compiler_params=_cparams(("parallel",)),
        name="merge",
    )(x, o_ret, s5_pre, o_ssd, p_main, w_ro, w_glu, w_so, w_out)


def _seq_index(nseq):
    r = ROWS // nseq
    i = np.arange(ROWS)
    return i // r, i % r, r


def _block_masks(nseq):
    s, t, _ = _seq_index(nseq)
    same = s[:, None] == s[None, :]
    causal = same & (t[None, :] <= t[:, None])
    return jnp.asarray(causal, F32), jnp.asarray(same, F32)


def _retention_tables(nseq):
    s, t, r = _seq_index(nseq)
    lg = jnp.log1p(-jnp.exp2(-5.0 - jnp.arange(RET_HEADS, dtype=F32)))[:, None, None]
    tf = jnp.asarray(t, F32)
    diff = tf[:, None] - tf[None, :]
    causal = jnp.asarray((s[:, None] == s[None, :]) & (t[None, :] <= t[:, None]))
    decay = jnp.where(causal[None], jnp.exp(jnp.where(causal, diff, 0.0)[None] * lg), 0.0)
    qdec = jnp.broadcast_to(jnp.exp((tf + 1.0)[None, :, None] * lg), (RET_HEADS, ROWS, RET_DV))
    kdec = jnp.broadcast_to(jnp.exp((r - 1.0 - tf)[None, :, None] * lg), (RET_HEADS, ROWS, RET_DK))
    cdec = jnp.broadcast_to(jnp.exp(r * lg), (RET_HEADS, RET_DK, RET_DV))
    return decay, qdec, kdec, cdec


def _rotary_tables(pos):
    half = RET_DK // 2
    inv = ROPE_BASE ** (-jnp.arange(half, dtype=F32) / half)
    ang = pos.astype(F32)[:, None] * inv[None, :]
    cos, sin = jnp.cos(ang), jnp.sin(ang)
    return jnp.concatenate([cos, cos], axis=1), jnp.concatenate([-sin, sin], axis=1)


def _ret_body(q_ref, k_ref, v_ref, g_ref, cos_ref, sin_ref, dec_ref, qd_ref, kd_ref, cd_ref, lng_ref,
              s0_ref, o_ref, s_ref, q_scr, kd_scr, cross_scr, *, nseq, hps):
    r = ROWS // nseq

    @pl.when(pl.program_id(2) == 0)
    def _():
        s_ref[...] = s0_ref[...]

    cos = cos_ref[...]
    sin = sin_ref[...]

    def rot(x):
        return x * cos + pltpu.roll(x, RET_DK // 2, 1) * sin

    for h in range(hps):
        ks = slice(h * RET_DK, (h + 1) * RET_DK)
        vs = slice(h * RET_DV, (h + 1) * RET_DV)
        q = rot(q_ref[:, ks])
        k = rot(k_ref[:, ks]) * (RET_DK ** -0.5)
        qb = q.astype(BF16)
        vb = v_ref[:, vs].astype(BF16)
        scores = _dot_nt(qb, k.astype(BF16)) * dec_ref[h]
        inner = _dot(scores.astype(BF16), vb)
        kd = k * kd_ref[h]
        cd = cd_ref[h]
        if nseq == 1:
            s = s_ref[0, h]
            cross = _dot(qb, s.astype(BF16))
            s_ref[0, h] = s * cd + _dot_tn(kd.astype(BF16), vb)
        else:
            q_scr[...] = q
            kd_scr[...] = kd

            def seq_step(i, carry, h=h, vs=vs, cd=cd):
                rs = pl.ds(pl.multiple_of(i * r, r), r)
                s = s_ref[i, h]
                cross_scr[rs, :] = _dot(q_scr[rs, :].astype(BF16), s.astype(BF16))
                s_ref[i, h] = s * cd + _dot_tn(kd_scr[rs, :].astype(BF16), v_ref[rs, vs].astype(BF16))
                return carry

            lax.fori_loop(0, nseq, seq_step, 0, unroll=4)
            cross = cross_scr[...]
        o = inner + cross * qd_ref[h]
        mu = jnp.mean(o, axis=-1, keepdims=True)
        oc = o - mu
        var = jnp.mean(oc * oc, axis=-1, keepdims=True)
        on = (oc * lax.rsqrt(var + EPS)) * lng_ref[:, vs]
        o_ref[:, vs] = (_silu(g_ref[:, vs]) * on).astype(o_ref.dtype)


def _retention(p_main, nblk, nc, nseq, hps, cos, sin, tabs, ln_g, s0):
    h = RET_HEADS
    hb = h // hps
    decay, qdec, kdec, cdec = tabs
    kw, vw = hps * RET_DK, hps * RET_DV

    in_specs = [
        pl.BlockSpec((ROWS, kw), lambda b, hh, c: (b * nc + c, P_Q // kw + hh)),
        pl.BlockSpec((ROWS, kw), lambda b, hh, c: (b * nc + c, P_K // kw + hh)),
        pl.BlockSpec((ROWS, vw), lambda b, hh, c: (b * nc + c, P_V // vw + hh)),
        pl.BlockSpec((ROWS, vw), lambda b, hh, c: (b * nc + c, P_G // vw + hh)),
        pl.BlockSpec((ROWS, RET_DK), lambda b, hh, c: (c, 0)),
        pl.BlockSpec((ROWS, RET_DK), lambda b, hh, c: (c, 0)),
        pl.BlockSpec((hps, ROWS, ROWS), lambda b, hh, c: (hh, 0, 0)),
        pl.BlockSpec((hps, ROWS, RET_DV), lambda b, hh, c: (hh, 0, 0)),
        pl.BlockSpec((hps, ROWS, RET_DK), lambda b, hh, c: (hh, 0, 0)),
        pl.BlockSpec((hps, RET_DK, RET_DV), lambda b, hh, c: (hh, 0, 0)),
        pl.BlockSpec((1, vw), lambda b, hh, c: (0, hh)),
        pl.BlockSpec((nseq, hps, RET_DK, RET_DV), lambda b, hh, c: (b, hh, 0, 0)),
    ]
    out_specs = [
        pl.BlockSpec((ROWS, vw), lambda b, hh, c: (b * nc + c, hh)),
        pl.BlockSpec((nseq, hps, RET_DK, RET_DV), lambda b, hh, c: (b, hh, 0, 0)),
    ]
    out_shape = [
        jax.ShapeDtypeStruct((nblk * nc * ROWS, h * RET_DV), BF16),
        jax.ShapeDtypeStruct((nblk * nseq, h, RET_DK, RET_DV), F32),
    ]
    return pl.pallas_call(
        functools.partial(_ret_body, nseq=nseq, hps=hps),
        grid=(nblk, hb, nc),
        in_specs=in_specs, out_specs=out_specs, out_shape=out_shape,
        scratch_shapes=[pltpu.VMEM((ROWS, RET_DK), F32), pltpu.VMEM((ROWS, RET_DK), F32),
                        pltpu.VMEM((ROWS, RET_DV), F32)],
        compiler_params=_cparams(("parallel", "parallel", "arbitrary")),
        name="retention",
    )(p_main, p_main, p_main, p_main, cos, sin, decay, qdec, kdec, cdec, ln_g.reshape(1, -1), s0)


def _s5_disc_body(ar_ref, ai_ref, ldt_ref, br_ref, bi_ref, abr_ref, abi_ref, bbr_ref, bbi_ref):
    ar = ar_ref[...]
    ai = ai_ref[...]
    dt = jnp.exp(ldt_ref[...])
    mag = jnp.exp(dt * ar)
    abr = mag * jnp.cos(dt * ai)
    abi = mag * jnp.sin(dt * ai)
    nr = abr - 1.0
    ni = abi
    den = ar * ar + ai * ai
    f_re = (nr * ar + ni * ai) / den
    f_im = (ni * ar - nr * ai) / den
    br = br_ref[...]
    bi = bi_ref[...]
    abr_ref[...] = abr
    abi_ref[...] = abi
    bbr_ref[...] = f_re * br - f_im * bi
    bbi_ref[...] = f_re * bi + f_im * br


def _s5_discretize(a_re, a_im, log_dt, b_re, b_im):
    g, n, c = b_re.shape
    flat = (g * c * n // 128, 128)

    def rep(a):
        return jnp.broadcast_to(a[:, None, :], (g, c, n)).reshape(flat)

    args = (rep(a_re), rep(a_im), rep(jnp.broadcast_to(log_dt[:, None], (g, n))),
            jnp.swapaxes(b_re, 1, 2).reshape(flat), jnp.swapaxes(b_im, 1, 2).reshape(flat))
    spec = pl.BlockSpec(flat, lambda: (0, 0))
    abr, abi, bbr, bbi = pl.pallas_call(
        _s5_disc_body,
        in_specs=[spec] * 5, out_specs=[spec] * 4,
        out_shape=[jax.ShapeDtypeStruct(flat, F32)] * 4,
        name="s5_discretize",
    )(*args)
    abr = abr.reshape(g, c, n)[:, 0, :].reshape(1, g * n)
    abi = abi.reshape(g, c, n)[:, 0, :].reshape(1, g * n)
    return abr, abi, bbr.reshape(g, c, n), bbi.reshape(g, c, n)


def _s5_block_weights(bb_re, bb_im, c_re, c_im):
    g, c, n = bb_re.shape
    ng = 16
    eye = jnp.eye(ng, dtype=F32)

    def b_blk(bb):
        x = bb.reshape(g // ng, ng, c, n)
        return jnp.einsum("kgcn,gh->kgchn", x, eye).reshape(g // ng, ng * c, ng * n).astype(BF16)

    def c_blk(cc):
        x = cc.reshape(g // ng, ng, c, n)
        return jnp.einsum("kgcn,gh->kgnhc", x, eye).reshape(g // ng, ng * n, ng * c).astype(BF16)

    return b_blk(bb_re), b_blk(bb_im), c_blk(c_re), c_blk(c_im)


def _s5_body(u_ref, bre_ref, bim_ref, cre_ref, cim_ref, ar_ref, ai_ref, d_ref, s0r_ref, s0i_ref,
             y_ref, sr_ref, si_ref, bur, bui, *, tl, bb):
    n_cg = bre_ref.shape[0]
    cw = bre_ref.shape[1]
    sw = bre_ref.shape[2]
    lc = 512

    @pl.when(pl.program_id(1) == 0)
    def _():
        sr_ref[...] = s0r_ref[...]
        si_ref[...] = s0i_ref[...]

    u = u_ref[...]
    ub = u.astype(BF16)
    for cg in range(n_cg):
        ucg = ub[:, cg * cw:(cg + 1) * cw]
        bur[:, cg * sw:(cg + 1) * sw] = _dot(ucg, bre_ref[cg])
        bui[:, cg * sw:(cg + 1) * sw] = _dot(ucg, bim_ref[cg])

    for bt in range(bb // 8):
        rs = slice(bt * 8, (bt + 1) * 8)
        for ci in range(n_cg * sw // lc):
            cs = slice(ci * lc, (ci + 1) * lc)
            ar = jnp.broadcast_to(ar_ref[:, cs], (8, lc))
            ai = jnp.broadcast_to(ai_ref[:, cs], (8, lc))

            def step(l, carry, cs=cs, ar=ar, ai=ai, bt=bt):
                xr, xi = carry
                r0 = pl.multiple_of(l * bb + bt * 8, 8)
                nr = (ar * xr - ai * xi) + bur[pl.ds(r0, 8), cs]
                ni = (ar * xi + ai * xr) + bui[pl.ds(r0, 8), cs]
                bur[pl.ds(r0, 8), cs] = nr
                bui[pl.ds(r0, 8), cs] = ni
                return nr, ni

            xr, xi = lax.fori_loop(0, tl, step, (sr_ref[rs, cs], si_ref[rs, cs]))
            sr_ref[rs, cs] = xr
            si_ref[rs, cs] = xi

    for cg in range(n_cg):
        xs = slice(cg * sw, (cg + 1) * sw)
        us = slice(cg * cw, (cg + 1) * cw)
        y = _dot(bur[:, xs].astype(BF16), cre_ref[cg]) - _dot(bui[:, xs].astype(BF16), cim_ref[cg])
        y = y + d_ref[:, us] * u[:, us]
        y_ref[:, us] = jax.nn.gelu(y).astype(y_ref.dtype)


def _s5(u_tm, wts, abr, abi, d, s0r, s0i, tl, bb):
    rows_total, w = u_tm.shape
    b, ns = s0r.shape
    bre, bim, cre, cim = wts
    rows = tl * bb
    nbb = b // bb
    ntb = rows_total // (rows * nbb)
    return pl.pallas_call(
        functools.partial(_s5_body, tl=tl, bb=bb),
        grid=(nbb, ntb),
        in_specs=[
            pl.BlockSpec((rows, w), lambda i, j: (i * ntb + j, 0)),
            _resident(bre.shape), _resident(bim.shape), _resident(cre.shape), _resident(cim.shape),
            _resident((1, ns)), _resident((1, ns)), _resident((1, w)),
            pl.BlockSpec((bb, ns), lambda i, j: (i, 0)),
            pl.BlockSpec((bb, ns), lambda i, j: (i, 0)),
        ],
        out_specs=[
            pl.BlockSpec((rows, w), lambda i, j: (i * ntb + j, 0)),
            pl.BlockSpec((bb, ns), lambda i, j: (i, 0)),
            pl.BlockSpec((bb, ns), lambda i, j: (i, 0)),
        ],
        out_shape=[
            jax.ShapeDtypeStruct((rows_total, w), BF16),
            jax.ShapeDtypeStruct((b, ns), F32),
            jax.ShapeDtypeStruct((b, ns), F32),
        ],
        scratch_shapes=[pltpu.VMEM((rows, ns), F32), pltpu.VMEM((rows, ns), F32)],
        compiler_params=_cparams(("parallel", "arbitrary")),
        name="s5",
    )(u_tm, bre, bim, cre, cim, abr, abi, d.reshape(1, w), s0r, s0i)


def _to_time_major(u, b, l, bb):
    w = u.shape[-1]
    return u.reshape(b // bb, bb, l, w).transpose(0, 2, 1, 3).reshape(b * l, w)


def _from_time_major(y, b, l, bb):
    w = y.shape[-1]
    return y.reshape(b // bb, l, bb, w).transpose(0, 2, 1, 3).reshape(b * l, w)


def _ssd_body(z_ref, xs_ref, bm_ref, cm_ref, dt_ref, px_ref, pb_ref, pc_ref, wx_ref, wb_ref, wc_ref,
              bx_ref, bb_ref, bc_ref, dtb_ref, alog_ref, dsk_ref, ng_ref, tri_ref, seqm_ref, h0_ref,
              o_ref, h_ref, tx, tb, tc, xsw_scr, bm_scr, cm_scr, ecl_scr, yoff_scr, *, nseq):
    r = ROWS // nseq
    pd = SSD_HEADDIM
    gw = SSD_HPG * pd
    row = lax.broadcasted_iota(jnp.int32, (ROWS, 1), 0)
    t_idx = row & (r - 1)
    lo = lax.broadcasted_iota(jnp.int32, (1, ROWS), 1) < pd

    @pl.when(pl.program_id(2) == 0)
    def _():
        h_ref[...] = h0_ref[...]
        if nseq == 1:
            tx[...] = px_ref[0]
            tb[...] = pb_ref[0]
            tc[...] = pc_ref[0]

    def conv(x_ref, tail, prev_ref, w_ref, b_ref):
        x = x_ref[...]
        w = x.shape[1]
        if nseq == 1:
            z8 = jnp.concatenate([tail[...], jnp.zeros((ROWS - 8, w), F32)], axis=0)
        else:
            z8 = prev_ref[...].reshape(ROWS, w)
        acc = b_ref[...] + w_ref[SSD_CONV - 1:SSD_CONV, :] * x
        for k in range(1, SSD_CONV):
            xk = jnp.where(t_idx >= k, pltpu.roll(x, k, 0), pltpu.roll(z8, ROWS - 8 + k, 0))
            acc = acc + w_ref[SSD_CONV - 1 - k:SSD_CONV - k, :] * xk
        if nseq == 1:
            tail[...] = x[ROWS - 8:, :]
        return _silu(acc)

    xs = conv(xs_ref, tx, px_ref, wx_ref, bx_ref)
    bm = conv(bm_ref, tb, pb_ref, wb_ref, bb_ref)
    cm = conv(cm_ref, tc, pc_ref, wc_ref, bc_ref)
    bmb = bm.astype(BF16)
    cmb = cm.astype(BF16)
    xsb = xs.astype(BF16)

    dt = _softplus(dt_ref[...] + dtb_ref[...])
    dta = dt * (-jnp.exp(alog_ref[...]))
    tri = tri_ref[...]
    cum = _dot_f32(tri, dta)
    clast = _dot_f32(seqm_ref[...], dta)
    cum_t = cum.T
    dt_t = dt.T
    e = jnp.exp(cum)
    dtw = dt * jnp.exp(clast - cum)
    ecl = jnp.exp(clast)
    causal = tri > 0.0
    cb = _dot_nt(cmb, bmb)

    def expand(a):
        return jnp.concatenate(
            [jnp.where(lo, a[:, 2 * p:2 * p + 1], a[:, 2 * p + 1:2 * p + 2]) for p in range(SSD_HPG // 2)], axis=1)

    ys = []
    zero = jnp.zeros((), BF16)
    for p in range(SSD_HPG // 2):
        ms = []
        for h in (2 * p, 2 * p + 1):
            seg = cum[:, h:h + 1] - cum_t[h:h + 1, :]
            lm = jnp.where(causal, jnp.exp(jnp.where(causal, seg, 0.0)), 0.0)
            ms.append(((cb * lm) * dt_t[h:h + 1, :]).astype(BF16))
        xp = xsb[:, p * 2 * pd:(p + 1) * 2 * pd]
        xcat = jnp.concatenate([jnp.where(lo, xp, zero), jnp.where(lo, zero, xp)], axis=0)
        ys.append(_dot(jnp.concatenate(ms, axis=1), xcat))
    y = jnp.concatenate(ys, axis=1)

    xsw = xs * expand(dtw)

    def new_state(hg, ecl_row, upd):
        parts = [hg[h * pd:(h + 1) * pd, :] * ecl_row[:, h:h + 1] for h in range(SSD_HPG)]
        return (jnp.concatenate(parts, axis=0) + upd).reshape(SSD_HPG, pd, SSD_STATE)

    if nseq == 1:
        hg = h_ref[0].reshape(gw, SSD_STATE)
        yoff = _dot_nt(cmb, hg.astype(BF16))
        h_ref[0] = new_state(hg, ecl[0:1, :], _dot_tn(xsw.astype(BF16), bmb))
    else:
        xsw_scr[...] = xsw
        bm_scr[...] = bm
        cm_scr[...] = cm
        ecl_scr[...] = ecl

        def seq_step(i, carry):
            rs = pl.ds(pl.multiple_of(i * r, r), r)
            hg = h_ref[i].reshape(gw, SSD_STATE)
            yoff_scr[rs, :] = _dot_nt(cm_scr[rs, :].astype(BF16), hg.astype(BF16))
            upd = _dot_tn(xsw_scr[rs, :].astype(BF16), bm_scr[rs, :].astype(BF16))
            h_ref[i] = new_state(hg, ecl_scr[pl.ds(pl.multiple_of(i * r, r), 1), :], upd)
            return carry

        lax.fori_loop(0, nseq, seq_step, 0, unroll=2)
        yoff = yoff_scr[...]

    y = y + yoff * expand(e) + dsk_ref[...] * xs
    y = y * _silu(z_ref[...])
    o_ref[...] = (_rms(y) * ng_ref[...]).astype(o_ref.dtype)


def _ssd(p_main, p_dt, nblk, nc, nseq, conv0, conv_w, conv_b, dtb, alog, dskip, norm_g, h0):
    g = SSD_GROUPS
    gw = SSD_INNER // g
    ns = SSD_STATE
    tri, seqm = _block_masks(nseq)
    xo = P_XBC
    bo = P_XBC + SSD_INNER
    co = bo + g * ns

    def cols(width, off):
        return lambda b, gg, c: (0, off // width + gg)

    in_specs = [
        pl.BlockSpec((ROWS, gw), lambda b, gg, c: (b * nc + c, P_Z // gw + gg)),
        pl.BlockSpec((ROWS, gw), lambda b, gg, c: (b * nc + c, xo // gw + gg)),
        pl.BlockSpec((ROWS, ns), lambda b, gg, c: (b * nc + c, bo // ns + gg)),
        pl.BlockSpec((ROWS, ns), lambda b, gg, c: (b * nc + c, co // ns + gg)),
        pl.BlockSpec((ROWS, 128), lambda b, gg, c: (b * nc + c, gg)),
        pl.BlockSpec((nseq, 8, gw), lambda b, gg, c: (b, 0, gg)),
        pl.BlockSpec((nseq, 8, ns), lambda b, gg, c: (b, 0, SSD_INNER // ns + gg)),
        pl.BlockSpec((nseq, 8, ns), lambda b, gg, c: (b, 0, SSD_INNER // ns + g + gg)),
        pl.BlockSpec((SSD_CONV, gw), cols(gw, 0)),
        pl.BlockSpec((SSD_CONV, ns), cols(ns, SSD_INNER)),
        pl.BlockSpec((SSD_CONV, ns), cols(ns, SSD_INNER + g * ns)),
        pl.BlockSpec((1, gw), cols(gw, 0)),
        pl.BlockSpec((1, ns), cols(ns, SSD_INNER)),
        pl.BlockSpec((1, ns), cols(ns, SSD_INNER + g * ns)),
        pl.BlockSpec((1, 128), cols(128, 0)),
        pl.BlockSpec((1, 128), cols(128, 0)),
        pl.BlockSpec((1, gw), cols(gw, 0)),
        pl.BlockSpec((1, gw), cols(gw, 0)),
        pl.BlockSpec((ROWS, ROWS), lambda b, gg, c: (0, 0)),
        pl.BlockSpec((ROWS, ROWS), lambda b, gg, c: (0, 0)),
        pl.BlockSpec((nseq, SSD_HPG, SSD_HEADDIM, ns), lambda b, gg, c: (b, gg, 0, 0)),
    ]
    out_specs = [
        pl.BlockSpec((ROWS, gw), lambda b, gg, c: (b * nc + c, gg)),
        pl.BlockSpec((nseq, SSD_HPG, SSD_HEADDIM, ns), lambda b, gg, c: (b, gg, 0, 0)),
    ]
    out_shape = [
        jax.ShapeDtypeStruct((nblk * nc * ROWS, SSD_INNER), BF16),
        jax.ShapeDtypeStruct((nblk * nseq, SSD_HEADS, SSD_HEADDIM, ns), F32),
    ]
    scratch = [pltpu.VMEM((8, gw), F32), pltpu.VMEM((8, ns), F32), pltpu.VMEM((8, ns), F32),
               pltpu.VMEM((ROWS, gw), F32), pltpu.VMEM((ROWS, ns), F32), pltpu.VMEM((ROWS, ns), F32),
               pltpu.VMEM((ROWS, 128), F32), pltpu.VMEM((ROWS, gw), F32)]
    return pl.pallas_call(
        functools.partial(_ssd_body, nseq=nseq),
        grid=(nblk, g, nc),
        in_specs=in_specs, out_specs=out_specs, out_shape=out_shape,
        scratch_shapes=scratch,
        compiler_params=_cparams(("parallel", "parallel", "arbitrary")),
        name="ssd",
    )(p_main, p_main, p_main, p_main, p_dt, conv0, conv0, conv0, conv_w, conv_w, conv_w,
      conv_b, conv_b, conv_b, dtb, alog, dskip, norm_g, tri, seqm, h0)


def _group_pad(v):
    return jnp.pad(v.reshape(SSD_GROUPS, SSD_HPG), ((0, 0), (0, 128 - SSD_HPG))).reshape(1, -1)


def _prep_w_in(w_in):
    o_dt = P_XBC + SSD_CONV_DIM
    w_main = jnp.concatenate([w_in[:, :o_dt], w_in[:, o_dt + SSD_HEADS:]], axis=1).astype(BF16)
    w_dt = w_in[:, o_dt:o_dt + SSD_HEADS].reshape(-1, SSD_GROUPS, SSD_HPG)
    w_dt = jnp.pad(w_dt, ((0, 0), (0, 0), (0, 128 - SSD_HPG))).reshape(-1, SSD_GROUPS * 128).astype(BF16)
    return w_main, w_dt


def _pad_conv(c):
    return jnp.pad(c, ((0, 0), (8 - (SSD_CONV - 1), 0), (0, 0)))


class _Stream:
    def __init__(self, b, l, pos0, s5_tl, s5_bb, ret_hps):
        self.b, self.l = b, l
        self.nseq = 1 if l % ROWS == 0 else ROWS // l
        self.nc = max(l // ROWS, 1)
        self.nblk = b // self.nseq
        self.s5_tl, self.s5_bb, self.ret_hps = s5_tl, s5_bb, ret_hps
        pos = pos0 + jnp.arange(self.nc * ROWS, dtype=jnp.int32) % l
        self.cos, self.sin = _rotary_tables(pos)
        self.rtab = _retention_tables(self.nseq)


def _layer(x, st, w, ret0, s5r0, s5i0, ssm0, conv0, last):
    b, l = st.b, st.l
    d = x.shape[1]
    x, h_mix = _ffn(x, w["ffn1_norm"], w["ffn1_wg"], w["ffn1_wu"], w["ffn1_wd"], w["mix_norm"], BF16)
    p_main = _matmul(h_mix, w["w_main"])
    p_dt = _matmul(h_mix, w["w_dt"])

    o_ret, ret_new = _retention(p_main, st.nblk, st.nc, st.nseq, st.ret_hps, st.cos, st.sin, st.rtab,
                                w["ret_ln_g"], ret0)

    u_tm = _to_time_major(p_main[:, P_U:P_U + d], b, l, st.s5_bb)
    y5, s5r, s5i = _s5(u_tm, w["s5_wts"], w["s5_abr"], w["s5_abi"], w["s5_d"], s5r0, s5i0, st.s5_tl, st.s5_bb)
    s5_pre = _from_time_major(y5, b, l, st.s5_bb)

    o_ssd, ssm_new = _ssd(p_main, p_dt, st.nblk, st.nc, st.nseq, conv0, w["conv_w"], w["conv_b"],
                          w["dtb"], w["alog"], w["dskip"], w["ssd_norm"], ssm0)
    conv_new = p_main.reshape(b, l, -1)[:, l - (SSD_CONV - 1):, P_XBC:P_XBC + SSD_CONV_DIM]

    x = _merge(x, o_ret, s5_pre, o_ssd, p_main, w["ret_wo"], w["s5_wglu"], w["ssd_wo"], w["w_out"])
    x, y_fin = _ffn(x, w["ffn2_norm"], w["ffn2_wg"], w["ffn2_wu"], w["ffn2_wd"], w["final_norm"],
                    F32 if last else BF16)
    s5_new = jnp.stack([s5r, s5i], axis=-1).reshape(b, S5_GROUPS, S5_STATE, 2)
    return x, y_fin, ret_new, s5_new, ssm_new, conv_new


def kernel(x_prompt, x_sample, state_ret, state_s5, state_ssm, state_conv, ffn1_norm, ffn1_w_gu, ffn1_w_down, mix_norm, w_in, ret_ln_g, ret_w_o, s5_a_re, s5_a_im, s5_log_dt, s5_b_re, s5_b_im, s5_c_re, s5_c_im, s5_d, s5_w_glu, ssd_conv_w, ssd_conv_b, ssd_dt_bias, ssd_a_log, ssd_d, ssd_norm, ssd_w_o, w_out, ffn2_norm, ffn2_w_gu, ffn2_w_down, final_norm):
    depth = w_in.shape[0]
    bp, lp, d = x_prompt.shape
    bs, ls, _ = x_sample.shape
    assert lp % ROWS == 0 and ROWS % ls == 0 and bs % (ROWS // ls) == 0 and bp % 8 == 0 and bs % 32 == 0
    st_p = _Stream(bp, lp, 0, s5_tl=32, s5_bb=bp, ret_hps=RET_HEADS)
    st_s = _Stream(bs, ls, PAST_LEN, s5_tl=ls, s5_bb=32, ret_hps=1)

    xp = x_prompt.reshape(bp * lp, d)
    xs = x_sample.reshape(bs * ls, d)
    zero_ret = jnp.zeros((bp, RET_HEADS, RET_DK, RET_DV), F32)
    zero_s5 = jnp.zeros((bp, S5_GROUPS * S5_STATE), F32)
    zero_ssm = jnp.zeros((bp, SSD_HEADS, SSD_HEADDIM, SSD_STATE), F32)
    zero_conv = jnp.zeros((bp, 8, SSD_CONV_DIM), F32)

    outs = [[] for _ in range(8)]
    yp = ys = None
    f = ffn1_w_gu.shape[2] // 2
    for l in range(depth):
        abr, abi, bbr, bbi = _s5_discretize(s5_a_re[l], s5_a_im[l], s5_log_dt[l], s5_b_re[l], s5_b_im[l])
        w_main, w_dt = _prep_w_in(w_in[l])
        w = dict(
            ffn1_norm=ffn1_norm[l], ffn1_wg=ffn1_w_gu[l, :, :f].astype(BF16), ffn1_wu=ffn1_w_gu[l, :, f:].astype(BF16),
            ffn1_wd=ffn1_w_down[l].astype(BF16), mix_norm=mix_norm[l], w_main=w_main, w_dt=w_dt,
            ret_ln_g=ret_ln_g[l], s5_wts=_s5_block_weights(bbr, bbi, s5_c_re[l], s5_c_im[l]), s5_abr=abr, s5_abi=abi,
            s5_d=s5_d[l], conv_w=ssd_conv_w[l], conv_b=ssd_conv_b[l].reshape(1, -1), dtb=_group_pad(ssd_dt_bias[l]),
            alog=_group_pad(ssd_a_log[l]), dskip=jnp.repeat(ssd_d[l], SSD_HEADDIM).reshape(1, SSD_INNER),
            ssd_norm=ssd_norm[l].reshape(1, SSD_INNER), ret_wo=ret_w_o[l].astype(BF16),
            s5_wglu=s5_w_glu[l].astype(BF16), ssd_wo=ssd_w_o[l].astype(BF16), w_out=w_out[l].astype(BF16),
            ffn2_norm=ffn2_norm[l], ffn2_wg=ffn2_w_gu[l, :, :f].astype(BF16), ffn2_wu=ffn2_w_gu[l, :, f:].astype(BF16),
            ffn2_wd=ffn2_w_down[l].astype(BF16), final_norm=final_norm,
        )
        last = l == depth - 1
        xp, yp, r1, s1, m1, c1 = _layer(xp, st_p, w, zero_ret, zero_s5, zero_s5, zero_ssm, zero_conv, last)
        ss = state_s5[l].reshape(bs, S5_GROUPS * S5_STATE, 2)
        xs, ys, r2, s2, m2, c2 = _layer(xs, st_s, w, state_ret[l], ss[..., 0], ss[..., 1], state_ssm[l],
                                        _pad_conv(state_conv[l]), last)
        for lst, v in zip(outs, (r1, r2, s1, s2, m1, m2, c1, c2)):
            lst.append(v)

    return (yp.reshape(bp, lp, d), ys.reshape(bs, ls, d)) + tuple(jnp.stack(o) for o in outs)
```

```python
import functools

import numpy as np
import jax
import jax.numpy as jnp
from jax import lax
from jax.experimental import pallas as pl
from jax.experimental.pallas import tpu as pltpu

F32 = jnp.float32
BF16 = jnp.bfloat16

D_MODEL = 1024
PAST_LEN = 16384
EPS = 1e-6
RET_HEADS = 4
RET_DK = 128
RET_DV = 256
ROPE_BASE = 10000.0
S5_GROUPS = 64
S5_STATE = 64
SSD_INNER = 2048
SSD_HEADDIM = 64
SSD_HEADS = 32
SSD_GROUPS = 4
SSD_HPG = 8
SSD_STATE = 128
SSD_CONV = 4
SSD_CONV_DIM = 3072

ROWS = 128
VMEM_LIMIT = 56 * 1024 * 1024

P_Q, P_K, P_V, P_G, P_GATE, P_Z, P_XBC = 0, 512, 1024, 2048, 3072, 6144, 8192
W_U, W_Z, W_DT, W_GATE = 3072, 4096, 9216, 9248


def _cparams(sem):
    return pltpu.CompilerParams(dimension_semantics=sem, vmem_limit_bytes=VMEM_LIMIT)


def _pick(n, cands):
    for c in cands:
        if n % c == 0:
            return c
    raise ValueError(f"no tile for {n}")


def _resident(shape):
    nd = len(shape)
    return pl.BlockSpec(shape, lambda *_: (0,) * nd, pipeline_mode=pl.Buffered(1))


def _rms(x):
    return x * lax.rsqrt(jnp.mean(x * x, axis=-1, keepdims=True) + EPS)


def _silu(x):
    return x * jax.nn.sigmoid(x)


def _softplus(x):
    return jnp.maximum(x, 0.0) + jnp.log1p(jnp.exp(-jnp.abs(x)))


def _dot(a, b):
    return jnp.dot(a, b, preferred_element_type=F32)


def _dot_nt(a, b):
    return lax.dot_general(a, b, (((1,), (1,)), ((), ())), preferred_element_type=F32)


def _dot_tn(a, b):
    return lax.dot_general(a, b, (((0,), (0,)), ((), ())), preferred_element_type=F32)


def _dot_f32(a, b):
    return jnp.dot(a, b, preferred_element_type=F32, precision=lax.Precision.HIGHEST)


def _ffn_body(x_ref, ng_ref, wg_ref, wu_ref, wd_ref, *rest, n_chunks, tf):
    o_ref = rest[-2] if len(rest) == 3 else rest[0]
    x = x_ref[...]
    hb = (_rms(x) * ng_ref[...]).astype(BF16)
    acc = jnp.zeros(x.shape, F32)
    for c in range(n_chunks):
        sl = slice(c * tf, (c + 1) * tf)
        g = _dot(hb, wg_ref[:, sl])
        u = _dot(hb, wu_ref[:, sl])
        acc = acc + _dot((_silu(g) * u).astype(BF16), wd_ref[sl, :])
    y = x + 0.5 * acc
    o_ref[...] = y
    if len(rest) == 3:
        pg_ref, _, h_ref = rest
        h_ref[...] = (_rms(y) * pg_ref[...]).astype(h_ref.dtype)


def _ffn(x, norm_g, w_g, w_u, w_d, post_g=None, post_dtype=None):
    t, d = x.shape
    f = w_g.shape[1]
    tm = _pick(t, (512, 256, 128))
    n_chunks = 2
    tf = f // n_chunks
    row = pl.BlockSpec((1, d), lambda i: (0, 0))
    tile = pl.BlockSpec((tm, d), lambda i: (i, 0))
    with_post = post_g is not None
    res = pl.pallas_call(
        functools.partial(_ffn_body, n_chunks=n_chunks, tf=tf),
        grid=(t // tm,),
        in_specs=[tile, row, _resident((d, f)), _resident((d, f)), _resident((f, d))] + [row] * with_post,
        out_specs=[tile] + [tile] * with_post,
        out_shape=[jax.ShapeDtypeStruct((t, d), F32)] + [jax.ShapeDtypeStruct((t, d), post_dtype)] * with_post,
        compiler_params=_cparams(("parallel",)),
        name="ffn",
    )(x, norm_g.reshape(1, d), w_g, w_u, w_d, *([post_g.reshape(1, d)] if with_post else []))
    return res if with_post else (res[0], None)


def _mm_body(a_ref, w_ref, o_ref):
    o_ref[...] = _dot(a_ref[...], w_ref[...])


def _matmul(a, w, seq_len=None):
    t, k = a.shape
    n = w.shape[1]
    tm = _pick(seq_len or t, (2048, 1024, 512, 256, 128))
    tn = _pick(n, (1024, 512, 256, 128))
    if seq_len is None:
        out_spec = pl.BlockSpec((tm, tn), lambda i, j: (i, j))
        out_shape = (t, n)
    else:
        assert tn == n
        nt = seq_len // tm
        out_spec = pl.BlockSpec((tm, n), lambda i, j: (i % nt, i // nt))
        out_shape = (seq_len, (t // seq_len) * n)
    out = pl.pallas_call(
        _mm_body,
        grid=(t // tm, n // tn),
        in_specs=[pl.BlockSpec((tm, k), lambda i, j: (i, 0)), pl.BlockSpec((k, tn), lambda i, j: (0, j))],
        out_specs=out_spec,
        out_shape=jax.ShapeDtypeStruct(out_shape, F32),
        compiler_params=_cparams(("parallel", "parallel")),
        name="in_proj",
    )(a, w)
    return out.reshape(t, n)


def _merge_body(x_ref, oret_ref, s5_ref, ossd_ref, gl_ref, wro_ref, wglu_ref, wso_ref, wout_ref, o_ref):
    d = D_MODEL
    y_ret = _dot(oret_ref[...], wro_ref[...])
    yag = _dot(s5_ref[...], wglu_ref[...])
    y_s5 = yag[:, :d] * jax.nn.sigmoid(yag[:, d:])
    y_ssd = _dot(ossd_ref[...], wso_ref[...])
    gl = gl_ref[...]
    merged = (jax.nn.sigmoid(gl[:, :d]) * y_ret + jax.nn.sigmoid(gl[:, d:2 * d]) * y_s5
              + jax.nn.sigmoid(gl[:, 2 * d:]) * y_ssd)
    o_ref[...] = x_ref[...] + _dot(merged.astype(BF16), wout_ref[...])


def _merge(x, o_ret, s5_pre, o_ssd, p_main, w_ro, w_glu, w_so, w_out, s5_seq_len=None):
    t, d = x.shape
    tm = _pick(s5_seq_len or t, (512, 256, 128))
    if s5_seq_len is None:
        s5_spec = pl.BlockSpec((tm, d), lambda i: (i, 0))
    else:
        nt = s5_seq_len // tm
        s5_pre = s5_pre.reshape(s5_seq_len, (t // s5_seq_len) * d)
        s5_spec = pl.BlockSpec((tm, d), lambda i: (i % nt, i // nt))
    return pl.pallas_call(
        _merge_body,
        grid=(t // tm,),
        in_specs=[
            pl.BlockSpec((tm, d), lambda i: (i, 0)),
            pl.BlockSpec((tm, d), lambda i: (i, 0)),
            s5_spec,
            pl.BlockSpec((tm, 2 * d), lambda i: (i, 0)),
            pl.BlockSpec((tm, 3 * d), lambda i: (i, P_GATE // (3 * d))),
            _resident((d, d)), _resident((d, 2 * d)), _resident((2 * d, d)), _resident((d, d)),
        ],
        out_specs=pl.BlockSpec((tm, d), lambda i: (i, 0)),
        out_shape=jax.ShapeDtypeStruct((t, d), F32),
        compiler_params=_cparams(("parallel",)),
        name="merge",
    )(x, o_ret, s5_pre, o_ssd, p_main, w_ro, w_glu, w_so, w_out)


def _seq_index(nseq):
    r = ROWS // nseq
    i = np.arange(ROWS)
    return i // r, i % r, r


def _block_masks(nseq):
    s, t, _ = _seq_index(nseq)
    same = s[:, None] == s[None, :]
    causal = same & (t[None, :] <= t[:, None])
    return jnp.asarray(causal, F32), jnp.asarray(same, F32)


def _retention_tables(nseq):
    s, t, r = _seq_index(nseq)
    lg = jnp.log1p(-jnp.exp2(-5.0 - jnp.arange(RET_HEADS, dtype=F32)))[:, None, None]
    tf = jnp.asarray(t, F32)
    diff = tf[:, None] - tf[None, :]
    causal = jnp.asarray((s[:, None] == s[None, :]) & (t[None, :] <= t[:, None]))
    decay = jnp.where(causal[None], jnp.exp(jnp.where(causal, diff, 0.0)[None] * lg), 0.0)
    qdec = jnp.broadcast_to(jnp.exp((tf + 1.0)[None, :, None] * lg), (RET_HEADS, ROWS, RET_DV))
    kdec = jnp.broadcast_to(jnp.exp((r - 1.0 - tf)[None, :, None] * lg), (RET_HEADS, ROWS, RET_DK))
    cdec = jnp.broadcast_to(jnp.exp(r * lg), (RET_HEADS, RET_DK, RET_DV))
    return decay, qdec, kdec, cdec


def _rotary_tables(pos):
    half = RET_DK // 2
    inv = ROPE_BASE ** (-jnp.arange(half, dtype=F32) / half)
    ang = pos.astype(F32)[:, None] * inv[None, :]
    cos, sin = jnp.cos(ang), jnp.sin(ang)
    return jnp.concatenate([cos, cos], axis=1), jnp.concatenate([-sin, sin], axis=1)


def _ret_body(q_ref, k_ref, v_ref, g_ref, cos_ref, sin_ref, dec_ref, qd_ref, kd_ref, cd_ref, lng_ref,
              s0_ref, *rest, nseq, hps):
    o_ref, s_ref, q_scr, kd_scr, cross_scr = rest[-5:]
    r = ROWS // nseq

    @pl.when(pl.program_id(2) == 0)
    def _():
        s_ref[...] = s0_ref[...]

    cos = cos_ref[...]
    sin = sin_ref[...]

    def rot(x):
        return x * cos + pltpu.roll(x, RET_DK // 2, 1) * sin

    for h in range(hps):
        ks = slice(h * RET_DK, (h + 1) * RET_DK)
        vs = slice(h * RET_DV, (h + 1) * RET_DV)
        q = rot(q_ref[:, ks])
        k = rot(k_ref[:, ks]) * (RET_DK ** -0.5)
        qb = q.astype(BF16)
        vb = v_ref[:, vs].astype(BF16)
        scores = _dot_nt(qb, k.astype(BF16)) * dec_ref[h]
        inner = _dot(scores.astype(BF16), vb)
        kd = k * kd_ref[h]
        cd = cd_ref[h]
        if nseq == 1:
            s = s_ref[0, h]
            cross = _dot(qb, s.astype(BF16))
            s_ref[0, h] = s * cd + _dot_tn(kd.astype(BF16), vb)
        else:
            q_scr[...] = q
            kd_scr[...] = kd

            def seq_step(i, carry, h=h, vs=vs, cd=cd):
                rs = pl.ds(pl.multiple_of(i * r, r), r)
                s = s_ref[i, h]
                cross_scr[rs, :] = _dot(q_scr[rs, :].astype(BF16), s.astype(BF16))
                s_ref[i, h] = s * cd + _dot_tn(kd_scr[rs, :].astype(BF16), v_ref[rs, vs].astype(BF16))
                return carry

            lax.fori_loop(0, nseq, seq_step, 0, unroll=4)
            cross = cross_scr[...]
        o = inner + cross * qd_ref[h]
        mu = jnp.mean(o, axis=-1, keepdims=True)
        oc = o - mu
        var = jnp.mean(oc * oc, axis=-1, keepdims=True)
        on = (oc * lax.rsqrt(var + EPS)) * lng_ref[:, vs]
        o_ref[:, vs] = (_silu(g_ref[:, vs]) * on).astype(o_ref.dtype)


def _stacked_state_io(state_all, lin, out_prev, lout, n_out, blk, idx):
    in_spec = pl.BlockSpec((None,) + blk, lambda *g: (lin,) + idx(*g))
    out_spec = pl.BlockSpec((None,) + blk, lambda *g: (lout,) + idx(*g))
    out_shape = jax.ShapeDtypeStruct((n_out,) + state_all.shape[1:], state_all.dtype)
    extra_specs = [] if out_prev is None else [pl.BlockSpec(memory_space=pl.ANY)]
    extra_args = [] if out_prev is None else [out_prev]
    return in_spec, out_spec, out_shape, extra_specs, extra_args


def _retention(p_main, nblk, nc, nseq, hps, cos, sin, tabs, ln_g, s0_all, lin, out_prev, lout, n_out):
    h = RET_HEADS
    hb = h // hps
    decay, qdec, kdec, cdec = tabs
    kw, vw = hps * RET_DK, hps * RET_DV
    s_in, s_out, s_shape, x_specs, x_args = _stacked_state_io(
        s0_all, lin, out_prev, lout, n_out, (nseq, hps, RET_DK, RET_DV), lambda b, hh, c: (b, hh, 0, 0))

    in_specs = [
        pl.BlockSpec((ROWS, kw), lambda b, hh, c: (b * nc + c, P_Q // kw + hh)),
        pl.BlockSpec((ROWS, kw), lambda b, hh, c: (b * nc + c, P_K // kw + hh)),
        pl.BlockSpec((ROWS, vw), lambda b, hh, c: (b * nc + c, P_V // vw + hh)),
        pl.BlockSpec((ROWS, vw), lambda b, hh, c: (b * nc + c, P_G // vw + hh)),
        pl.BlockSpec((ROWS, RET_DK), lambda b, hh, c: (c, 0)),
        pl.BlockSpec((ROWS, RET_DK), lambda b, hh, c: (c, 0)),
        pl.BlockSpec((hps, ROWS, ROWS), lambda b, hh, c: (hh, 0, 0)),
        pl.BlockSpec((hps, ROWS, RET_DV), lambda b, hh, c: (hh, 0, 0)),
        pl.BlockSpec((hps, ROWS, RET_DK), lambda b, hh, c: (hh, 0, 0)),
        pl.BlockSpec((hps, RET_DK, RET_DV), lambda b, hh, c: (hh, 0, 0)),
        pl.BlockSpec((1, vw), lambda b, hh, c: (0, hh)),
        s_in,
    ] + x_specs
    out_specs = [pl.BlockSpec((ROWS, vw), lambda b, hh, c: (b * nc + c, hh)), s_out]
    out_shape = [jax.ShapeDtypeStruct((nblk * nc * ROWS, h * RET_DV), BF16), s_shape]
    return pl.pallas_call(
        functools.partial(_ret_body, nseq=nseq, hps=hps),
        grid=(nblk, hb, nc),
        in_specs=in_specs, out_specs=out_specs, out_shape=out_shape,
        scratch_shapes=[pltpu.VMEM((ROWS, RET_DK), F32), pltpu.VMEM((ROWS, RET_DK), F32),
                        pltpu.VMEM((ROWS, RET_DV), F32)],
        input_output_aliases={len(in_specs) - 1: 1} if x_args else {},
        compiler_params=_cparams(("parallel", "parallel", "arbitrary")),
        name="retention",
    )(p_main, p_main, p_main, p_main, cos, sin, decay, qdec, kdec, cdec, ln_g.reshape(1, -1), s0_all, *x_args)


def _s5_disc_body(ar_ref, ai_ref, ldt_ref, br_ref, bi_ref, abr_ref, abi_ref, bbr_ref, bbi_ref):
    ar = ar_ref[...]
    ai = ai_ref[...]
    dt = jnp.exp(ldt_ref[...])
    mag = jnp.exp(dt * ar)
    abr = mag * jnp.cos(dt * ai)
    abi = mag * jnp.sin(dt * ai)
    nr = abr - 1.0
    ni = abi
    den = ar * ar + ai * ai
    f_re = (nr * ar + ni * ai) / den
    f_im = (ni * ar - nr * ai) / den
    br = br_ref[...]
    bi = bi_ref[...]
    abr_ref[...] = abr
    abi_ref[...] = abi
    bbr_ref[...] = f_re * br - f_im * bi
    bbi_ref[...] = f_re * bi + f_im * br


def _s5_discretize(a_re, a_im, log_dt, b_re, b_im):
    g, n, c = b_re.shape
    flat = (g * c * n // 128, 128)

    def rep(a):
        return jnp.broadcast_to(a[:, None, :], (g, c, n)).reshape(flat)

    args = (rep(a_re), rep(a_im), rep(jnp.broadcast_to(log_dt[:, None], (g, n))),
            jnp.swapaxes(b_re, 1, 2).reshape(flat), jnp.swapaxes(b_im, 1, 2).reshape(flat))
    spec = pl.BlockSpec(flat, lambda: (0, 0))
    abr, abi, bbr, bbi = pl.pallas_call(
        _s5_disc_body,
        in_specs=[spec] * 5, out_specs=[spec] * 4,
        out_shape=[jax.ShapeDtypeStruct(flat, F32)] * 4,
        name="s5_discretize",
    )(*args)
    abr = abr.reshape(g, c, n)[:, 0, :].reshape(1, g * n)
    abi = abi.reshape(g, c, n)[:, 0, :].reshape(1, g * n)
    return abr, abi, bbr.reshape(g, c, n), bbi.reshape(g, c, n)


def _s5_block_weights(bb_re, bb_im, c_re, c_im):
    g, c, n = bb_re.shape
    ng = 16
    eye = jnp.eye(ng, dtype=F32)

    def b_blk(bb):
        x = bb.reshape(g // ng, ng, c, n)
        return jnp.einsum("kgcn,gh->kgchn", x, eye).reshape(g // ng, ng * c, ng * n).astype(BF16)

    def c_blk(cc):
        x = cc.reshape(g // ng, ng, c, n)
        return jnp.einsum("kgcn,gh->kgnhc", x, eye).reshape(g // ng, ng * n, ng * c).astype(BF16)

    return b_blk(bb_re), b_blk(bb_im), c_blk(c_re), c_blk(c_im)


def _s5_body(u_ref, bre_ref, bim_ref, cre_ref, cim_ref, ar_ref, ai_ref, d_ref, s0r_ref, s0i_ref,
             y_ref, sr_ref, si_ref, bur, bui, *, tl, bb):
    n_cg = bre_ref.shape[0]
    cw = bre_ref.shape[1]
    sw = bre_ref.shape[2]
    lc = 512

    @pl.when(pl.program_id(1) == 0)
    def _():
        sr_ref[...] = s0r_ref[...]
        si_ref[...] = s0i_ref[...]

    u = u_ref[...]
    ub = u.astype(BF16)
    for cg in range(n_cg):
        ucg = ub[:, cg * cw:(cg + 1) * cw]
        bur[:, cg * sw:(cg + 1) * sw] = _dot(ucg, bre_ref[cg])
        bui[:, cg * sw:(cg + 1) * sw] = _dot(ucg, bim_ref[cg])

    for bt in range(bb // 8):
        rs = slice(bt * 8, (bt + 1) * 8)
        for ci in range(n_cg * sw // lc):
            cs = slice(ci * lc, (ci + 1) * lc)
            ar = jnp.broadcast_to(ar_ref[:, cs], (8, lc))
            ai = jnp.broadcast_to(ai_ref[:, cs], (8, lc))

            def step(l, carry, cs=cs, ar=ar, ai=ai, bt=bt):
                xr, xi = carry
                r0 = pl.multiple_of(l * bb + bt * 8, 8)
                nr = (ar * xr - ai * xi) + bur[pl.ds(r0, 8), cs]
                ni = (ar * xi + ai * xr) + bui[pl.ds(r0, 8), cs]
                bur[pl.ds(r0, 8), cs] = nr
                bui[pl.ds(r0, 8), cs] = ni
                return nr, ni

            xr, xi = lax.fori_loop(0, tl, step, (sr_ref[rs, cs], si_ref[rs, cs]))
            sr_ref[rs, cs] = xr
            si_ref[rs, cs] = xi

    for cg in range(n_cg):
        xs = slice(cg * sw, (cg + 1) * sw)
        us = slice(cg * cw, (cg + 1) * cw)
        y = _dot(bur[:, xs].astype(BF16), cre_ref[cg]) - _dot(bui[:, xs].astype(BF16), cim_ref[cg])
        y = y + d_ref[:, us] * u[:, us]
        y_ref[:, us] = jax.nn.gelu(y).astype(y_ref.dtype)


def _s5(u_tm, wts, abr, abi, d, s0r, s0i, tl, bb):
    rows_total, w = u_tm.shape
    b, ns = s0r.shape
    bre, bim, cre, cim = wts
    rows = tl * bb
    nbb = b // bb
    ntb = rows_total // (rows * nbb)
    return pl.pallas_call(
        functools.partial(_s5_body, tl=tl, bb=bb),
        grid=(nbb, ntb),
        in_specs=[
            pl.BlockSpec((rows, w), lambda i, j: (i * ntb + j, 0)),
            _resident(bre.shape), _resident(bim.shape), _resident(cre.shape), _resident(cim.shape),
            _resident((1, ns)), _resident((1, ns)), _resident((1, w)),
            pl.BlockSpec((bb, ns), lambda i, j: (i, 0)),
            pl.BlockSpec((bb, ns), lambda i, j: (i, 0)),
        ],
        out_specs=[
            pl.BlockSpec((rows, w), lambda i, j: (i * ntb + j, 0)),
            pl.BlockSpec((bb, ns), lambda i, j: (i, 0)),
            pl.BlockSpec((bb, ns), lambda i, j: (i, 0)),
        ],
        out_shape=[
            jax.ShapeDtypeStruct((rows_total, w), BF16),
            jax.ShapeDtypeStruct((b, ns), F32),
            jax.ShapeDtypeStruct((b, ns), F32),
        ],
        scratch_shapes=[pltpu.VMEM((rows, ns), F32), pltpu.VMEM((rows, ns), F32)],
        compiler_params=_cparams(("parallel", "arbitrary")),
        name="s5",
    )(u_tm, bre, bim, cre, cim, abr, abi, d.reshape(1, w), s0r, s0i)


def _to_time_major(u, b, l, bb):
    w = u.shape[-1]
    return u.reshape(b // bb, bb, l, w).transpose(0, 2, 1, 3).reshape(b * l, w)


def _from_time_major(y, b, l, bb):
    w = y.shape[-1]
    return y.reshape(b // bb, l, bb, w).transpose(0, 2, 1, 3).reshape(b * l, w)


def _ssd_body(z_ref, xs_ref, bm_ref, cm_ref, dt_ref, px_ref, pb_ref, pc_ref, wx_ref, wb_ref, wc_ref,
              bx_ref, bb_ref, bc_ref, dtb_ref, alog_ref, dsk_ref, ng_ref, tri_ref, trit_ref, seqm_ref, h0_ref,
              *rest, nseq, gps):
    o_ref, h_ref, tx, tb, tc, xsw_scr, bm_scr, cm_scr, col_scr, yoff_scr = rest[-10:]

    @pl.when(pl.program_id(2) == 0)
    def _():
        h_ref[...] = h0_ref[...]
        if nseq == 1:
            tx[...] = px_ref[0]
            tb[...] = pb_ref[0]
            tc[...] = pc_ref[0]

    for gi in range(gps):
        _ssd_group(gi, z_ref, xs_ref, bm_ref, cm_ref, dt_ref, px_ref, pb_ref, pc_ref, wx_ref, wb_ref, wc_ref,
                   bx_ref, bb_ref, bc_ref, dtb_ref, alog_ref, dsk_ref, ng_ref, tri_ref, trit_ref, seqm_ref,
                   o_ref, h_ref, tx, tb, tc, xsw_scr, bm_scr, cm_scr, col_scr, yoff_scr, nseq)


def _ssd_group(gi, z_ref, xs_ref, bm_ref, cm_ref, dt_ref, px_ref, pb_ref, pc_ref, wx_ref, wb_ref, wc_ref,
               bx_ref, bb_ref, bc_ref, dtb_ref, alog_ref, dsk_ref, ng_ref, tri_ref, trit_ref, seqm_ref,
               o_ref, h_ref, tx, tb, tc, xsw_scr, bm_scr, cm_scr, col_scr, yoff_scr, nseq):
    r = ROWS // nseq
    pd = SSD_HEADDIM
    nh = SSD_HPG
    gw = nh * pd
    gc = slice(gi * gw, (gi + 1) * gw)
    gn = slice(gi * SSD_STATE, (gi + 1) * SSD_STATE)
    gh = slice(gi * nh, (gi + 1) * nh)
    row = lax.broadcasted_iota(jnp.int32, (ROWS, 1), 0)
    t_idx = row & (r - 1)
    lo = lax.broadcasted_iota(jnp.int32, (1, ROWS), 1) < pd

    def conv(x_ref, tail, prev_ref, w_ref, b_ref, cs):
        x = x_ref[:, cs]
        acc = b_ref[:, cs] + w_ref[SSD_CONV - 1:SSD_CONV, cs] * x
        if nseq == 1:
            xe = jnp.concatenate([tail[:, cs], x], axis=0)
            tail[:, cs] = x[ROWS - 8:, :]
        else:
            z8 = prev_ref[:, :, cs].reshape(ROWS, x.shape[1])
        for k in range(1, SSD_CONV):
            if nseq == 1:
                xk = xe[8 - k:8 - k + ROWS, :]
            else:
                xk = jnp.where(t_idx >= k, pltpu.roll(x, k, 0), pltpu.roll(z8, ROWS - 8 + k, 0))
            acc = acc + w_ref[SSD_CONV - 1 - k:SSD_CONV - k, cs] * xk
        return _silu(acc)

    xs = conv(xs_ref, tx, px_ref, wx_ref, bx_ref, gc)
    bm = conv(bm_ref, tb, pb_ref, wb_ref, bb_ref, gn)
    cm = conv(cm_ref, tc, pc_ref, wc_ref, bc_ref, gn)
    bmb = bm.astype(BF16)
    cmb = cm.astype(BF16)
    xsb = xs.astype(BF16)

    dt8 = _softplus(dt_ref[:, gn].T[0:nh, :] + dtb_ref[gh, :])
    dta8 = dt8 * (-jnp.exp(alog_ref[gh, :]))
    cum8 = _dot_f32(dta8, trit_ref[...])
    clast8 = _dot_f32(dta8, seqm_ref[...])
    e8 = jnp.exp(cum8)
    dtw8 = dt8 * jnp.exp(clast8 - cum8)
    ecl8 = jnp.exp(clast8)
    colf = jnp.concatenate([cum8, e8, dtw8, ecl8, jnp.zeros((ROWS - 4 * nh, ROWS), F32)], axis=0).T
    c_cum, c_e, c_dtw, c_ecl = 0, nh, 2 * nh, 3 * nh
    causal = tri_ref[...] > 0.0
    cb = _dot_nt(cmb, bmb)

    def expand(off):
        return jnp.concatenate(
            [jnp.where(lo, colf[:, off + 2 * p:off + 2 * p + 1], colf[:, off + 2 * p + 1:off + 2 * p + 2])
             for p in range(nh // 2)], axis=1)

    ys = []
    zero = jnp.zeros((), BF16)
    for p in range(nh // 2):
        ms = []
        for h in (2 * p, 2 * p + 1):
            seg = colf[:, c_cum + h:c_cum + h + 1] - cum8[h:h + 1, :]
            lm = jnp.where(causal, jnp.exp(seg), 0.0)
            ms.append(((cb * lm) * dt8[h:h + 1, :]).astype(BF16))
        xp = xsb[:, p * 2 * pd:(p + 1) * 2 * pd]
        xcat = jnp.concatenate([jnp.where(lo, xp, zero), jnp.where(lo, zero, xp)], axis=0)
        ys.append(_dot(jnp.concatenate(ms, axis=1), xcat))
    y = jnp.concatenate(ys, axis=1)

    xsw = xs * expand(c_dtw)

    def new_state(hg, col_row, upd):
        parts = [hg[h * pd:(h + 1) * pd, :] * col_row[:, c_ecl + h:c_ecl + h + 1] for h in range(nh)]
        return (jnp.concatenate(parts, axis=0) + upd).reshape(nh, pd, SSD_STATE)

    if nseq == 1:
        hg = h_ref[0, gh].reshape(gw, SSD_STATE)
        yoff = _dot_nt(cmb, hg.astype(BF16))
        h_ref[0, gh] = new_state(hg, colf[0:1, :], _dot_tn(xsw.astype(BF16), bmb))
    else:
        xsw_scr[...] = xsw
        bm_scr[...] = bm
        cm_scr[...] = cm
        col_scr[...] = colf

        def seq_step(i, carry):
            r0 = pl.multiple_of(i * r, r)
            rs = pl.ds(r0, r)
            hg = h_ref[i, gh].reshape(gw, SSD_STATE)
            yoff_scr[rs, :] = _dot_nt(cm_scr[rs, :].astype(BF16), hg.astype(BF16))
            upd = _dot_tn(xsw_scr[rs, :].astype(BF16), bm_scr[rs, :].astype(BF16))
            h_ref[i, gh] = new_state(hg, col_scr[pl.ds(r0, 1), :], upd)
            return carry

        lax.fori_loop(0, nseq, seq_step, 0, unroll=2)
        yoff = yoff_scr[...]

    y = y + yoff * expand(c_e) + dsk_ref[:, gc] * xs
    y = y * _silu(z_ref[:, gc])
    o_ref[:, gc] = (_rms(y) * ng_ref[:, gc]).astype(o_ref.dtype)


def _ssd(p_main, p_dt, nblk, nc, nseq, gps, conv0, conv_w, conv_b, dtb, alog, dskip, norm_g,
         h0_all, lin, out_prev, lout, n_out):
    g = SSD_GROUPS // gps
    gw = gps * SSD_INNER // SSD_GROUPS
    ns = gps * SSD_STATE
    nh = gps * SSD_HPG
    tri, seqm = _block_masks(nseq)
    h_in, h_out, h_shape, x_specs, x_args = _stacked_state_io(
        h0_all, lin, out_prev, lout, n_out, (nseq, nh, SSD_HEADDIM, SSD_STATE), lambda b, gg, c: (b, gg, 0, 0))
    xo = P_XBC
    bo = P_XBC + SSD_INNER
    co = bo + SSD_GROUPS * SSD_STATE

    def cols(width, off):
        return lambda b, gg, c: (0, off // width + gg)

    in_specs = [
        pl.BlockSpec((ROWS, gw), lambda b, gg, c: (b * nc + c, P_Z // gw + gg)),
        pl.BlockSpec((ROWS, gw), lambda b, gg, c: (b * nc + c, xo // gw + gg)),
        pl.BlockSpec((ROWS, ns), lambda b, gg, c: (b * nc + c, bo // ns + gg)),
        pl.BlockSpec((ROWS, ns), lambda b, gg, c: (b * nc + c, co // ns + gg)),
        pl.BlockSpec((ROWS, ns), lambda b, gg, c: (b * nc + c, gg)),
        pl.BlockSpec((nseq, 8, gw), lambda b, gg, c: (b, 0, gg)),
        pl.BlockSpec((nseq, 8, ns), lambda b, gg, c: (b, 0, (bo - xo) // ns + gg)),
        pl.BlockSpec((nseq, 8, ns), lambda b, gg, c: (b, 0, (co - xo) // ns + gg)),
        pl.BlockSpec((SSD_CONV, gw), cols(gw, 0)),
        pl.BlockSpec((SSD_CONV, ns), cols(ns, bo - xo)),
        pl.BlockSpec((SSD_CONV, ns), cols(ns, co - xo)),
        pl.BlockSpec((1, gw), cols(gw, 0)),
        pl.BlockSpec((1, ns), cols(ns, bo - xo)),
        pl.BlockSpec((1, ns), cols(ns, co - xo)),
        pl.BlockSpec((nh, ROWS), lambda b, gg, c: (gg, 0)),
        pl.BlockSpec((nh, ROWS), lambda b, gg, c: (gg, 0)),
        pl.BlockSpec((1, gw), cols(gw, 0)),
        pl.BlockSpec((1, gw), cols(gw, 0)),
        pl.BlockSpec((ROWS, ROWS), lambda b, gg, c: (0, 0)),
        pl.BlockSpec((ROWS, ROWS), lambda b, gg, c: (0, 0)),
        pl.BlockSpec((ROWS, ROWS), lambda b, gg, c: (0, 0)),
        h_in,
    ] + x_specs
    out_specs = [pl.BlockSpec((ROWS, gw), lambda b, gg, c: (b * nc + c, gg)), h_out]
    out_shape = [jax.ShapeDtypeStruct((nblk * nc * ROWS, SSD_INNER), BF16), h_shape]
    assert nseq == 1 or gps == 1
    gw1 = SSD_INNER // SSD_GROUPS
    scratch = [pltpu.VMEM((8, gw), F32), pltpu.VMEM((8, ns), F32), pltpu.VMEM((8, ns), F32),
               pltpu.VMEM((ROWS, gw1), F32), pltpu.VMEM((ROWS, SSD_STATE), F32), pltpu.VMEM((ROWS, SSD_STATE), F32),
               pltpu.VMEM((ROWS, 128), F32), pltpu.VMEM((ROWS, gw1), F32)]
    return pl.pallas_call(
        functools.partial(_ssd_body, nseq=nseq, gps=gps),
        grid=(nblk, g, nc),
        in_specs=in_specs, out_specs=out_specs, out_shape=out_shape,
        scratch_shapes=scratch,
        input_output_aliases={len(in_specs) - 1: 1} if x_args else {},
        compiler_params=_cparams(("parallel", "parallel", "arbitrary")),
        name="ssd",
    )(p_main, p_main, p_main, p_main, p_dt, conv0, conv0, conv0, conv_w, conv_w, conv_w,
      conv_b, conv_b, conv_b, dtb, alog, dskip, norm_g, tri, tri.T, seqm, h0_all, *x_args)


def _head_rows(v):
    return jnp.broadcast_to(v[:, None], (SSD_HEADS, ROWS))


def _prep_w_in(w_in):
    w_main = jnp.concatenate([w_in[:, :W_U], w_in[:, W_GATE:], w_in[:, W_Z:W_DT]], axis=1).astype(BF16)
    w_u = w_in[:, W_U:W_Z].astype(BF16)
    w_dt = w_in[:, W_DT:W_GATE].reshape(-1, SSD_GROUPS, SSD_HPG)
    w_dt = jnp.pad(w_dt, ((0, 0), (0, 0), (0, 128 - SSD_HPG))).reshape(-1, SSD_GROUPS * 128).astype(BF16)
    return w_main, w_u, w_dt


def _pad_conv(c):
    return jnp.pad(c, ((0, 0), (8 - (SSD_CONV - 1), 0), (0, 0)))


class _Stream:
    def __init__(self, b, l, pos0, s5_tl, s5_bb, ret_hps, ssd_gps):
        self.b, self.l = b, l
        self.nseq = 1 if l % ROWS == 0 else ROWS // l
        self.nc = max(l // ROWS, 1)
        self.nblk = b // self.nseq
        self.s5_tl, self.s5_bb, self.ret_hps, self.ssd_gps = s5_tl, s5_bb, ret_hps, ssd_gps
        pos = pos0 + jnp.arange(self.nc * ROWS, dtype=jnp.int32) % l
        self.cos, self.sin = _rotary_tables(pos)
        self.rtab = _retention_tables(self.nseq)


def _layer(x, st, w, depth, lout, ret_all, ssm_all, lin, ret_prev, ssm_prev, s5r0, s5i0, conv0):
    b, l = st.b, st.l
    whole_seq = st.nseq == 1
    x, h_mix = _ffn(x, w["ffn1_norm"], w["ffn1_wg"], w["ffn1_wu"], w["ffn1_wd"], w["mix_norm"], BF16)
    p_main = _matmul(h_mix, w["w_main"])
    p_dt = _matmul(h_mix, w["w_dt"])

    o_ret, ret_out = _retention(p_main, st.nblk, st.nc, st.nseq, st.ret_hps, st.cos, st.sin, st.rtab,
                                w["ret_ln_g"], ret_all, lin, ret_prev, lout, depth)

    if whole_seq:
        u_tm = _matmul(h_mix, w["w_u"], seq_len=l)
    else:
        u_tm = _to_time_major(_matmul(h_mix, w["w_u"]), b, l, st.s5_bb)
    y5, s5r, s5i = _s5(u_tm, w["s5_wts"], w["s5_abr"], w["s5_abi"], w["s5_d"], s5r0, s5i0, st.s5_tl, st.s5_bb)
    s5_pre = y5 if whole_seq else _from_time_major(y5, b, l, st.s5_bb)

    o_ssd, ssm_out = _ssd(p_main, p_dt, st.nblk, st.nc, st.nseq, st.ssd_gps, conv0, w["conv_w"], w["conv_b"],
                          w["dtb"], w["alog"], w["dskip"], w["ssd_norm"], ssm_all, lin, ssm_prev, lout, depth)
    conv_new = p_main.reshape(b, l, -1)[:, l - (SSD_CONV - 1):, P_XBC:P_XBC + SSD_CONV_DIM]

    x = _merge(x, o_ret, s5_pre, o_ssd, p_main, w["ret_wo"], w["s5_wglu"], w["ssd_wo"], w["w_out"],
               s5_seq_len=l if whole_seq else None)
    x, y_fin = _ffn(x, w["ffn2_norm"], w["ffn2_wg"], w["ffn2_wu"], w["ffn2_wd"],
                    w["final_norm"] if lout == depth - 1 else None, F32)
    s5_new = jnp.stack([s5r, s5i], axis=-1).reshape(b, S5_GROUPS, S5_STATE, 2)
    return x, y_fin, ret_out, ssm_out, s5_new, conv_new


def kernel(x_prompt, x_sample, state_ret, state_s5, state_ssm, state_conv, ffn1_norm, ffn1_w_gu, ffn1_w_down, mix_norm, w_in, ret_ln_g, ret_w_o, s5_a_re, s5_a_im, s5_log_dt, s5_b_re, s5_b_im, s5_c_re, s5_c_im, s5_d, s5_w_glu, ssd_conv_w, ssd_conv_b, ssd_dt_bias, ssd_a_log, ssd_d, ssd_norm, ssd_w_o, w_out, ffn2_norm, ffn2_w_gu, ffn2_w_down, final_norm):
    depth = w_in.shape[0]
    bp, lp, d = x_prompt.shape
    bs, ls, _ = x_sample.shape
    assert lp % ROWS == 0 and ROWS % ls == 0 and bs % (ROWS // ls) == 0 and bp % 8 == 0 and bs % 32 == 0
    st_p = _Stream(bp, lp, 0, s5_tl=32, s5_bb=bp, ret_hps=RET_HEADS, ssd_gps=SSD_GROUPS)
    st_s = _Stream(bs, ls, PAST_LEN, s5_tl=ls, s5_bb=32, ret_hps=1, ssd_gps=1)

    xp = x_prompt.reshape(bp * lp, d)
    xs = x_sample.reshape(bs * ls, d)
    zero_ret = jnp.zeros((1, bp, RET_HEADS, RET_DK, RET_DV), F32)
    zero_s5 = jnp.zeros((bp, S5_GROUPS * S5_STATE), F32)
    zero_ssm = jnp.zeros((1, bp, SSD_HEADS, SSD_HEADDIM, SSD_STATE), F32)
    zero_conv = jnp.zeros((bp, 8, SSD_CONV_DIM), F32)

    small = [[] for _ in range(4)]
    yp = ys = ret_p = ret_s = ssm_p = ssm_s = None
    f = ffn1_w_gu.shape[2] // 2
    for l in range(depth):
        abr, abi, bbr, bbi = _s5_discretize(s5_a_re[l], s5_a_im[l], s5_log_dt[l], s5_b_re[l], s5_b_im[l])
        w_main, w_u, w_dt = _prep_w_in(w_in[l])
        w = dict(
            ffn1_norm=ffn1_norm[l], ffn1_wg=ffn1_w_gu[l, :, :f].astype(BF16), ffn1_wu=ffn1_w_gu[l, :, f:].astype(BF16),
            ffn1_wd=ffn1_w_down[l].astype(BF16), mix_norm=mix_norm[l], w_main=w_main, w_u=w_u, w_dt=w_dt,
            ret_ln_g=ret_ln_g[l], s5_wts=_s5_block_weights(bbr, bbi, s5_c_re[l], s5_c_im[l]), s5_abr=abr, s5_abi=abi,
            s5_d=s5_d[l], conv_w=ssd_conv_w[l], conv_b=ssd_conv_b[l].reshape(1, -1), dtb=_head_rows(ssd_dt_bias[l]),
            alog=_head_rows(ssd_a_log[l]), dskip=jnp.repeat(ssd_d[l], SSD_HEADDIM).reshape(1, SSD_INNER),
            ssd_norm=ssd_norm[l].reshape(1, SSD_INNER), ret_wo=ret_w_o[l].astype(BF16),
            s5_wglu=s5_w_glu[l].astype(BF16), ssd_wo=ssd_w_o[l].astype(BF16), w_out=w_out[l].astype(BF16),
            ffn2_norm=ffn2_norm[l], ffn2_wg=ffn2_w_gu[l, :, :f].astype(BF16), ffn2_wu=ffn2_w_gu[l, :, f:].astype(BF16),
            ffn2_wd=ffn2_w_down[l].astype(BF16), final_norm=final_norm,
        )
        xp, yp, ret_p, ssm_p, s1, c1 = _layer(xp, st_p, w, depth, l, zero_ret, zero_ssm, 0, ret_p, ssm_p,
                                              zero_s5, zero_s5, zero_conv)
        ss = state_s5[l].reshape(bs, S5_GROUPS * S5_STATE, 2)
        xs, ys, ret_s, ssm_s, s2, c2 = _layer(xs, st_s, w, depth, l, state_ret, state_ssm, l, ret_s, ssm_s,
                                              ss[..., 0], ss[..., 1], _pad_conv(state_conv[l]))
        for lst, v in zip(small, (s1, s2, c1, c2)):
            lst.append(v)

    s5_p, s5_s, conv_p, conv_s = (jnp.stack(o) for o in small)
    return (yp.reshape(bp, lp, d), ys.reshape(bs, ls, d), ret_p, ret_s, s5_p, s5_s, ssm_p, ssm_s, conv_p, conv_s)
```

```python
import functools

import numpy as np
import jax
import jax.numpy as jnp
from jax import lax
from jax.experimental import pallas as pl
from jax.experimental.pallas import tpu as pltpu

F32 = jnp.float32
BF16 = jnp.bfloat16

D_MODEL = 1024
PAST_LEN = 16384
EPS = 1e-6
RET_HEADS = 4
RET_DK = 128
RET_DV = 256
ROPE_BASE = 10000.0
S5_GROUPS = 64
S5_STATE = 64
SSD_INNER = 2048
SSD_HEADDIM = 64
SSD_HEADS = 32
SSD_GROUPS = 4
SSD_HPG = 8
SSD_STATE = 128
SSD_CONV = 4
SSD_CONV_DIM = 3072

ROWS = 128
VMEM_LIMIT = 56 * 1024 * 1024

P_Q, P_K, P_V, P_G, P_GATE, P_Z, P_XBC = 0, 512, 1024, 2048, 3072, 6144, 8192
W_U, W_Z, W_DT, W_GATE = 3072, 4096, 9216, 9248


def _cparams(sem):
    return pltpu.CompilerParams(dimension_semantics=sem, vmem_limit_bytes=VMEM_LIMIT)


def _pick(n, cands):
    for c in cands:
        if n % c == 0:
            return c
    raise ValueError(f"no tile for {n}")


def _resident(shape):
    nd = len(shape)
    return pl.BlockSpec(shape, lambda *_: (0,) * nd, pipeline_mode=pl.Buffered(1))


def _rms(x):
    return x * lax.rsqrt(jnp.mean(x * x, axis=-1, keepdims=True) + EPS)


def _silu(x):
    return x * jax.nn.sigmoid(x)


def _softplus(x):
    return jnp.maximum(x, 0.0) + jnp.log1p(jnp.exp(-jnp.abs(x)))


def _dot(a, b):
    return jnp.dot(a, b, preferred_element_type=F32)


def _dot_nt(a, b):
    return lax.dot_general(a, b, (((1,), (1,)), ((), ())), preferred_element_type=F32)


def _dot_tn(a, b):
    return lax.dot_general(a, b, (((0,), (0,)), ((), ())), preferred_element_type=F32)


def _dot_f32(a, b):
    return jnp.dot(a, b, preferred_element_type=F32, precision=lax.Precision.HIGHEST)


def _ffn_body(x_ref, ng_ref, wg_ref, wu_ref, wd_ref, *rest, n_chunks, tf):
    o_ref = rest[-2] if len(rest) == 3 else rest[0]
    x = x_ref[...]
    hb = (_rms(x) * ng_ref[...]).astype(BF16)
    acc = jnp.zeros(x.shape, F32)
    for c in range(n_chunks):
        sl = slice(c * tf, (c + 1) * tf)
        g = _dot(hb, wg_ref[:, sl])
        u = _dot(hb, wu_ref[:, sl])
        acc = acc + _dot((_silu(g) * u).astype(BF16), wd_ref[sl, :])
    y = x + 0.5 * acc
    o_ref[...] = y
    if len(rest) == 3:
        pg_ref, _, h_ref = rest
        h_ref[...] = (_rms(y) * pg_ref[...]).astype(h_ref.dtype)


def _ffn(x, norm_g, w_g, w_u, w_d, post_g=None, post_dtype=None):
    t, d = x.shape
    f = w_g.shape[1]
    tm = _pick(t, (512, 256, 128))
    n_chunks = 2
    tf = f // n_chunks
    row = pl.BlockSpec((1, d), lambda i: (0, 0))
    tile = pl.BlockSpec((tm, d), lambda i: (i, 0))
    with_post = post_g is not None
    res = pl.pallas_call(
        functools.partial(_ffn_body, n_chunks=n_chunks, tf=tf),
        grid=(t // tm,),
        in_specs=[tile, row, _resident((d, f)), _resident((d, f)), _resident((f, d))] + [row] * with_post,
        out_specs=[tile] + [tile] * with_post,
        out_shape=[jax.ShapeDtypeStruct((t, d), F32)] + [jax.ShapeDtypeStruct((t, d), post_dtype)] * with_post,
        compiler_params=_cparams(("parallel",)),
        name="ffn",
    )(x, norm_g.reshape(1, d), w_g, w_u, w_d, *([post_g.reshape(1, d)] if with_post else []))
    return res if with_post else (res[0], None)


def _mm_body(a_ref, w_ref, o_ref):
    o_ref[...] = _dot(a_ref[...], w_ref[...])


def _matmul(a, w, seq_len=None):
    t, k = a.shape
    n = w.shape[1]
    tm = _pick(seq_len or t, (2048, 1024, 512, 256, 128))
    tn = _pick(n, (1024, 512, 256, 128))
    if seq_len is None:
        out_spec = pl.BlockSpec((tm, tn), lambda i, j: (i, j))
        out_shape = (t, n)
    else:
        assert tn == n
        nt = seq_len // tm
        out_spec = pl.BlockSpec((tm, n), lambda i, j: (i % nt, i // nt))
        out_shape = (seq_len, (t // seq_len) * n)
    return pl.pallas_call(
        _mm_body,
        grid=(t // tm, n // tn),
        in_specs=[pl.BlockSpec((tm, k), lambda i, j: (i, 0)), pl.BlockSpec((k, tn), lambda i, j: (0, j))],
        out_specs=out_spec,
        out_shape=jax.ShapeDtypeStruct(out_shape, F32),
        compiler_params=_cparams(("parallel", "parallel")),
        name="in_proj",
    )(a, w)


def _merge_body(x_ref, oret_ref, s5_ref, ossd_ref, gl_ref, wro_ref, wglu_ref, wso_ref, wout_ref, o_ref):
    d = D_MODEL
    y_ret = _dot(oret_ref[...], wro_ref[...])
    yag = _dot(s5_ref[...], wglu_ref[...])
    y_s5 = yag[:, :d] * jax.nn.sigmoid(yag[:, d:])
    y_ssd = _dot(ossd_ref[...], wso_ref[...])
    gl = gl_ref[...]
    merged = (jax.nn.sigmoid(gl[:, :d]) * y_ret + jax.nn.sigmoid(gl[:, d:2 * d]) * y_s5
              + jax.nn.sigmoid(gl[:, 2 * d:]) * y_ssd)
    o_ref[...] = x_ref[...] + _dot(merged.astype(BF16), wout_ref[...])


def _merge(x, o_ret, s5_pre, o_ssd, p_main, w_ro, w_glu, w_so, w_out, s5_seq_len=None):
    t, d = x.shape
    tm = _pick(s5_seq_len or t, (512, 256, 128))
    if s5_seq_len is None:
        s5_spec = pl.BlockSpec((tm, d), lambda i: (i, 0))
    else:
        nt = s5_seq_len // tm
        s5_spec = pl.BlockSpec((tm, d), lambda i: (i % nt, i // nt))
    return pl.pallas_call(
        _merge_body,
        grid=(t // tm,),
        in_specs=[
            pl.BlockSpec((tm, d), lambda i: (i, 0)),
            pl.BlockSpec((tm, d), lambda i: (i, 0)),
            s5_spec,
            pl.BlockSpec((tm, 2 * d), lambda i: (i, 0)),
            pl.BlockSpec((tm, 3 * d), lambda i: (i, P_GATE // (3 * d))),
            _resident((d, d)), _resident((d, 2 * d)), _resident((2 * d, d)), _resident((d, d)),
        ],
        out_specs=pl.BlockSpec((tm, d), lambda i: (i, 0)),
        out_shape=jax.ShapeDtypeStruct((t, d), F32),
        compiler_params=_cparams(("parallel",)),
        name="merge",
    )(x, o_ret, s5_pre, o_ssd, p_main, w_ro, w_glu, w_so, w_out)


def _seq_index(nseq):
    r = ROWS // nseq
    i = np.arange(ROWS)
    return i // r, i % r, r


def _block_masks(nseq):
    s, t, _ = _seq_index(nseq)
    same = s[:, None] == s[None, :]
    causal = same & (t[None, :] <= t[:, None])
    return jnp.asarray(causal, F32), jnp.asarray(same, F32)


def _retention_tables(nseq):
    s, t, r = _seq_index(nseq)
    lg = jnp.log1p(-jnp.exp2(-5.0 - jnp.arange(RET_HEADS, dtype=F32)))[:, None, None]
    tf = jnp.asarray(t, F32)
    diff = tf[:, None] - tf[None, :]
    causal = jnp.asarray((s[:, None] == s[None, :]) & (t[None, :] <= t[:, None]))
    decay = jnp.where(causal[None], jnp.exp(jnp.where(causal, diff, 0.0)[None] * lg), 0.0)
    qdec = jnp.broadcast_to(jnp.exp((tf + 1.0)[None, :, None] * lg), (RET_HEADS, ROWS, RET_DV))
    kdec = jnp.broadcast_to(jnp.exp((r - 1.0 - tf)[None, :, None] * lg), (RET_HEADS, ROWS, RET_DK))
    cdec = jnp.broadcast_to(jnp.exp(r * lg), (RET_HEADS, RET_DK, RET_DV))
    return decay, qdec, kdec, cdec


def _rotary_tables(pos):
    half = RET_DK // 2
    inv = ROPE_BASE ** (-jnp.arange(half, dtype=F32) / half)
    ang = pos.astype(F32)[:, None] * inv[None, :]
    cos, sin = jnp.cos(ang), jnp.sin(ang)
    return jnp.concatenate([cos, cos], axis=1), jnp.concatenate([-sin, sin], axis=1)


def _ret_body(q_ref, k_ref, v_ref, g_ref, cos_ref, sin_ref, dec_ref, qd_ref, kd_ref, cd_ref, lng_ref,
              s0_ref, *rest, nseq, hps):
    o_ref, s_ref, q_scr, kd_scr, cross_scr = rest[-5:]
    r = ROWS // nseq

    @pl.when(pl.program_id(2) == 0)
    def _():
        s_ref[...] = s0_ref[...]

    cos = cos_ref[...]
    sin = sin_ref[...]

    def rot(x):
        return x * cos + pltpu.roll(x, RET_DK // 2, 1) * sin

    for h in range(hps):
        ks = slice(h * RET_DK, (h + 1) * RET_DK)
        vs = slice(h * RET_DV, (h + 1) * RET_DV)
        q = rot(q_ref[:, ks])
        k = rot(k_ref[:, ks]) * (RET_DK ** -0.5)
        qb = q.astype(BF16)
        vb = v_ref[:, vs].astype(BF16)
        scores = _dot_nt(qb, k.astype(BF16)) * dec_ref[h]
        inner = _dot(scores.astype(BF16), vb)
        kd = k * kd_ref[h]
        cd = cd_ref[h]
        if nseq == 1:
            s = s_ref[0, h]
            cross = _dot(qb, s.astype(BF16))
            s_ref[0, h] = s * cd + _dot_tn(kd.astype(BF16), vb)
        else:
            q_scr[...] = q
            kd_scr[...] = kd

            def seq_step(i, carry, h=h, vs=vs, cd=cd):
                rs = pl.ds(pl.multiple_of(i * r, r), r)
                s = s_ref[i, h]
                cross_scr[rs, :] = _dot(q_scr[rs, :].astype(BF16), s.astype(BF16))
                s_ref[i, h] = s * cd + _dot_tn(kd_scr[rs, :].astype(BF16), v_ref[rs, vs].astype(BF16))
                return carry

            lax.fori_loop(0, nseq, seq_step, 0, unroll=4)
            cross = cross_scr[...]
        o = inner + cross * qd_ref[h]
        mu = jnp.mean(o, axis=-1, keepdims=True)
        oc = o - mu
        var = jnp.mean(oc * oc, axis=-1, keepdims=True)
        on = (oc * lax.rsqrt(var + EPS)) * lng_ref[:, vs]
        o_ref[:, vs] = (_silu(g_ref[:, vs]) * on).astype(o_ref.dtype)


def _stacked_state_io(state_all, lin, out_prev, lout, n_out, blk, idx):
    in_spec = pl.BlockSpec((None,) + blk, lambda *g: (lin,) + idx(*g))
    out_spec = pl.BlockSpec((None,) + blk, lambda *g: (lout,) + idx(*g))
    out_shape = jax.ShapeDtypeStruct((n_out,) + state_all.shape[1:], state_all.dtype)
    assert out_prev.shape == out_shape.shape
    return in_spec, out_spec, out_shape, [pl.BlockSpec(memory_space=pl.ANY)], [out_prev]


def _retention(p_main, nblk, nc, nseq, hps, cos, sin, tabs, ln_g, s0_all, lin, out_prev, lout, n_out):
    h = RET_HEADS
    hb = h // hps
    decay, qdec, kdec, cdec = tabs
    kw, vw = hps * RET_DK, hps * RET_DV
    s_in, s_out, s_shape, x_specs, x_args = _stacked_state_io(
        s0_all, lin, out_prev, lout, n_out, (nseq, hps, RET_DK, RET_DV), lambda b, hh, c: (b, hh, 0, 0))

    in_specs = [
        pl.BlockSpec((ROWS, kw), lambda b, hh, c: (b * nc + c, P_Q // kw + hh)),
        pl.BlockSpec((ROWS, kw), lambda b, hh, c: (b * nc + c, P_K // kw + hh)),
        pl.BlockSpec((ROWS, vw), lambda b, hh, c: (b * nc + c, P_V // vw + hh)),
        pl.BlockSpec((ROWS, vw), lambda b, hh, c: (b * nc + c, P_G // vw + hh)),
        pl.BlockSpec((ROWS, RET_DK), lambda b, hh, c: (c, 0)),
        pl.BlockSpec((ROWS, RET_DK), lambda b, hh, c: (c, 0)),
        pl.BlockSpec((hps, ROWS, ROWS), lambda b, hh, c: (hh, 0, 0)),
        pl.BlockSpec((hps, ROWS, RET_DV), lambda b, hh, c: (hh, 0, 0)),
        pl.BlockSpec((hps, ROWS, RET_DK), lambda b, hh, c: (hh, 0, 0)),
        pl.BlockSpec((hps, RET_DK, RET_DV), lambda b, hh, c: (hh, 0, 0)),
        pl.BlockSpec((1, vw), lambda b, hh, c: (0, hh)),
        s_in,
    ] + x_specs
    out_specs = [pl.BlockSpec((ROWS, vw), lambda b, hh, c: (b * nc + c, hh)), s_out]
    out_shape = [jax.ShapeDtypeStruct((nblk * nc * ROWS, h * RET_DV), BF16), s_shape]
    return pl.pallas_call(
        functools.partial(_ret_body, nseq=nseq, hps=hps),
        grid=(nblk, hb, nc),
        in_specs=in_specs, out_specs=out_specs, out_shape=out_shape,
        scratch_shapes=[pltpu.VMEM((ROWS, RET_DK), F32), pltpu.VMEM((ROWS, RET_DK), F32),
                        pltpu.VMEM((ROWS, RET_DV), F32)],
        input_output_aliases={len(in_specs) - 1: 1} if x_args else {},
        compiler_params=_cparams(("parallel", "parallel", "arbitrary")),
        name="retention",
    )(p_main, p_main, p_main, p_main, cos, sin, decay, qdec, kdec, cdec, ln_g.reshape(1, -1), s0_all, *x_args)


def _s5_disc_body(ar_ref, ai_ref, ldt_ref, br_ref, bi_ref, abr_ref, abi_ref, bbr_ref, bbi_ref):
    ar = ar_ref[...]
    ai = ai_ref[...]
    dt = jnp.exp(ldt_ref[...])
    mag = jnp.exp(dt * ar)
    abr = mag * jnp.cos(dt * ai)
    abi = mag * jnp.sin(dt * ai)
    nr = abr - 1.0
    ni = abi
    den = ar * ar + ai * ai
    f_re = (nr * ar + ni * ai) / den
    f_im = (ni * ar - nr * ai) / den
    br = br_ref[...]
    bi = bi_ref[...]
    abr_ref[...] = abr
    abi_ref[...] = abi
    bbr_ref[...] = f_re * br - f_im * bi
    bbi_ref[...] = f_re * bi + f_im * br


def _s5_discretize(a_re, a_im, log_dt, b_re, b_im):
    g, n, c = b_re.shape
    flat = (g * c * n // 128, 128)

    def rep(a):
        return jnp.broadcast_to(a[:, None, :], (g, c, n)).reshape(flat)

    args = (rep(a_re), rep(a_im), rep(jnp.broadcast_to(log_dt[:, None], (g, n))),
            jnp.swapaxes(b_re, 1, 2).reshape(flat), jnp.swapaxes(b_im, 1, 2).reshape(flat))
    spec = pl.BlockSpec(flat, lambda: (0, 0))
    abr, abi, bbr, bbi = pl.pallas_call(
        _s5_disc_body,
        in_specs=[spec] * 5, out_specs=[spec] * 4,
        out_shape=[jax.ShapeDtypeStruct(flat, F32)] * 4,
        name="s5_discretize",
    )(*args)
    abr = abr.reshape(g, c, n)[:, 0, :].reshape(1, g * n)
    abi = abi.reshape(g, c, n)[:, 0, :].reshape(1, g * n)
    return abr, abi, bbr.reshape(g, c, n), bbi.reshape(g, c, n)


def _s5_block_weights(bb_re, bb_im, c_re, c_im):
    g, c, n = bb_re.shape
    ng = 16
    eye = jnp.eye(ng, dtype=F32)

    def b_blk(bb):
        x = bb.reshape(g // ng, ng, c, n)
        return jnp.einsum("kgcn,gh->kgchn", x, eye).reshape(g // ng, ng * c, ng * n).astype(BF16)

    def c_blk(cc):
        x = cc.reshape(g // ng, ng, c, n)
        return jnp.einsum("kgcn,gh->kgnhc", x, eye).reshape(g // ng, ng * n, ng * c).astype(BF16)

    return b_blk(bb_re), b_blk(bb_im), c_blk(c_re), c_blk(c_im)


def _s5_body(u_ref, bre_ref, bim_ref, cre_ref, cim_ref, ar_ref, ai_ref, d_ref, s0r_ref, s0i_ref,
             y_ref, sr_ref, si_ref, bur, bui, *, tl, bb, wide):
    n_cg = bre_ref.shape[0]
    cw = bre_ref.shape[1]
    sw = bre_ref.shape[2]
    w = n_cg * cw
    lc = 512

    @pl.when(pl.program_id(1) == 0)
    def _():
        sr_ref[...] = s0r_ref[...]
        si_ref[...] = s0i_ref[...]

    if wide:
        ux = u_ref[...]
        u = jnp.swapaxes(jnp.stack([ux[:, b * w:(b + 1) * w] for b in range(bb)], axis=0), 0, 1)
        u = u.reshape(tl * bb, w)
    else:
        u = u_ref[...]
    ub = u.astype(BF16)
    for cg in range(n_cg):
        ucg = ub[:, cg * cw:(cg + 1) * cw]
        bur[:, cg * sw:(cg + 1) * sw] = _dot(ucg, bre_ref[cg])
        bui[:, cg * sw:(cg + 1) * sw] = _dot(ucg, bim_ref[cg])

    for bt in range(bb // 8):
        rs = slice(bt * 8, (bt + 1) * 8)
        for ci in range(n_cg * sw // lc):
            cs = slice(ci * lc, (ci + 1) * lc)
            ar = jnp.broadcast_to(ar_ref[:, cs], (8, lc))
            ai = jnp.broadcast_to(ai_ref[:, cs], (8, lc))

            def step(l, carry, cs=cs, ar=ar, ai=ai, bt=bt):
                xr, xi = carry
                r0 = pl.multiple_of(l * bb + bt * 8, 8)
                nr = (ar * xr - ai * xi) + bur[pl.ds(r0, 8), cs]
                ni = (ar * xi + ai * xr) + bui[pl.ds(r0, 8), cs]
                bur[pl.ds(r0, 8), cs] = nr
                bui[pl.ds(r0, 8), cs] = ni
                return nr, ni

            xr, xi = lax.fori_loop(0, tl, step, (sr_ref[rs, cs], si_ref[rs, cs]), unroll=2)
            sr_ref[rs, cs] = xr
            si_ref[rs, cs] = xi

    for cg in range(n_cg):
        xs = slice(cg * sw, (cg + 1) * sw)
        us = slice(cg * cw, (cg + 1) * cw)
        y = _dot(bur[:, xs].astype(BF16), cre_ref[cg]) - _dot(bui[:, xs].astype(BF16), cim_ref[cg])
        y = jax.nn.gelu(y + d_ref[:, us] * u[:, us])
        if wide:
            y = jnp.swapaxes(y.reshape(tl, bb, cw), 0, 1)
            for b in range(bb):
                y_ref[:, b * w + cg * cw:b * w + (cg + 1) * cw] = y[b].astype(y_ref.dtype)
        else:
            y_ref[:, us] = y.astype(y_ref.dtype)


def _s5(u_tm, wts, abr, abi, d, s0r, s0i, tl, bb):
    b, ns = s0r.shape
    bre, bim, cre, cim = wts
    w = bre.shape[0] * bre.shape[1]
    wide = u_tm.shape[1] != w
    rows = tl * bb
    nbb = b // bb
    if wide:
        assert bb == b and u_tm.shape[1] == b * w
        ntb = u_tm.shape[0] // tl
        io_spec = pl.BlockSpec((tl, b * w), lambda i, j: (j, 0))
    else:
        ntb = u_tm.shape[0] // (rows * nbb)
        io_spec = pl.BlockSpec((rows, w), lambda i, j: (i * ntb + j, 0))
    return pl.pallas_call(
        functools.partial(_s5_body, tl=tl, bb=bb, wide=wide),
        grid=(nbb, ntb),
        in_specs=[
            io_spec,
            _resident(bre.shape), _resident(bim.shape), _resident(cre.shape), _resident(cim.shape),
            _resident((1, ns)), _resident((1, ns)), _resident((1, w)),
            pl.BlockSpec((bb, ns), lambda i, j: (i, 0)),
            pl.BlockSpec((bb, ns), lambda i, j: (i, 0)),
        ],
        out_specs=[
            io_spec,
            pl.BlockSpec((bb, ns), lambda i, j: (i, 0)),
            pl.BlockSpec((bb, ns), lambda i, j: (i, 0)),
        ],
        out_shape=[
            jax.ShapeDtypeStruct(u_tm.shape, BF16),
            jax.ShapeDtypeStruct((b, ns), F32),
            jax.ShapeDtypeStruct((b, ns), F32),
        ],
        scratch_shapes=[pltpu.VMEM((rows, ns), F32), pltpu.VMEM((rows, ns), F32)],
        compiler_params=_cparams(("parallel", "arbitrary")),
        name="s5",
    )(u_tm, bre, bim, cre, cim, abr, abi, d.reshape(1, w), s0r, s0i)


def _to_time_major(u, b, l, bb):
    w = u.shape[-1]
    return u.reshape(b // bb, bb, l, w).transpose(0, 2, 1, 3).reshape(b * l, w)


def _from_time_major(y, b, l, bb):
    w = y.shape[-1]
    return y.reshape(b // bb, l, bb, w).transpose(0, 2, 1, 3).reshape(b * l, w)


def _ssd_body(z_ref, xs_ref, bm_ref, cm_ref, dt_ref, px_ref, pb_ref, pc_ref, wx_ref, wb_ref, wc_ref,
              bx_ref, bb_ref, bc_ref, dtb_ref, alog_ref, dsk_ref, ng_ref, tri_ref, trit_ref, seqm_ref, h0_ref,
              *rest, nseq, gps):
    o_ref, h_ref, tx, tb, tc, xsw_scr, bm_scr, cm_scr, col_scr, yoff_scr = rest[-10:]

    @pl.when(pl.program_id(2) == 0)
    def _():
        h_ref[...] = h0_ref[...]
        if nseq == 1:
            tx[...] = px_ref[0]
            tb[...] = pb_ref[0]
            tc[...] = pc_ref[0]

    for gi in range(gps):
        _ssd_group(gi, z_ref, xs_ref, bm_ref, cm_ref, dt_ref, px_ref, pb_ref, pc_ref, wx_ref, wb_ref, wc_ref,
                   bx_ref, bb_ref, bc_ref, dtb_ref, alog_ref, dsk_ref, ng_ref, tri_ref, trit_ref, seqm_ref,
                   o_ref, h_ref, tx, tb, tc, xsw_scr, bm_scr, cm_scr, col_scr, yoff_scr, nseq)


def _ssd_group(gi, z_ref, xs_ref, bm_ref, cm_ref, dt_ref, px_ref, pb_ref, pc_ref, wx_ref, wb_ref, wc_ref,
               bx_ref, bb_ref, bc_ref, dtb_ref, alog_ref, dsk_ref, ng_ref, tri_ref, trit_ref, seqm_ref,
               o_ref, h_ref, tx, tb, tc, xsw_scr, bm_scr, cm_scr, col_scr, yoff_scr, nseq):
    r = ROWS // nseq
    pd = SSD_HEADDIM
    nh = SSD_HPG
    gw = nh * pd
    gc = slice(gi * gw, (gi + 1) * gw)
    gn = slice(gi * SSD_STATE, (gi + 1) * SSD_STATE)
    gh = slice(gi * nh, (gi + 1) * nh)
    row = lax.broadcasted_iota(jnp.int32, (ROWS, 1), 0)
    t_idx = row & (r - 1)
    lo = lax.broadcasted_iota(jnp.int32, (1, ROWS), 1) < pd

    def conv(x_ref, tail, prev_ref, w_ref, b_ref, cs):
        x = x_ref[:, cs]
        acc = b_ref[:, cs] + w_ref[SSD_CONV - 1:SSD_CONV, cs] * x
        if nseq == 1:
            xe = jnp.concatenate([tail[:, cs], x], axis=0)
            tail[:, cs] = x[ROWS - 8:, :]
        else:
            z8 = prev_ref[:, :, cs].reshape(ROWS, x.shape[1])
        for k in range(1, SSD_CONV):
            if nseq == 1:
                xk = xe[8 - k:8 - k + ROWS, :]
            else:
                xk = jnp.where(t_idx >= k, pltpu.roll(x, k, 0), pltpu.roll(z8, ROWS - 8 + k, 0))
            acc = acc + w_ref[SSD_CONV - 1 - k:SSD_CONV - k, cs] * xk
        return _silu(acc)

    xs = conv(xs_ref, tx, px_ref, wx_ref, bx_ref, gc)
    bm = conv(bm_ref, tb, pb_ref, wb_ref, bb_ref, gn)
    cm = conv(cm_ref, tc, pc_ref, wc_ref, bc_ref, gn)
    bmb = bm.astype(BF16)
    cmb = cm.astype(BF16)
    xsb = xs.astype(BF16)

    dt8 = _softplus(dt_ref[:, gn].T[0:nh, :] + dtb_ref[gh, :])
    dta8 = dt8 * (-jnp.exp(alog_ref[gh, :]))
    cum8 = _dot_f32(dta8, trit_ref[...])
    clast8 = _dot_f32(dta8, seqm_ref[...])
    e8 = jnp.exp(cum8)
    dtw8 = dt8 * jnp.exp(clast8 - cum8)
    ecl8 = jnp.exp(clast8)
    colf = jnp.concatenate([cum8, e8, dtw8, ecl8, jnp.zeros((ROWS - 4 * nh, ROWS), F32)], axis=0).T
    c_cum, c_e, c_dtw, c_ecl = 0, nh, 2 * nh, 3 * nh
    causal = tri_ref[...] > 0.0
    cb = _dot_nt(cmb, bmb)

    def expand(off):
        return jnp.concatenate(
            [jnp.where(lo, colf[:, off + 2 * p:off + 2 * p + 1], colf[:, off + 2 * p + 1:off + 2 * p + 2])
             for p in range(nh // 2)], axis=1)

    ys = []
    zero = jnp.zeros((), BF16)
    for p in range(nh // 2):
        ms = []
        for h in (2 * p, 2 * p + 1):
            seg = colf[:, c_cum + h:c_cum + h + 1] - cum8[h:h + 1, :]
            lm = jnp.where(causal, jnp.exp(seg), 0.0)
            ms.append(((cb * lm) * dt8[h:h + 1, :]).astype(BF16))
        xp = xsb[:, p * 2 * pd:(p + 1) * 2 * pd]
        xcat = jnp.concatenate([jnp.where(lo, xp, zero), jnp.where(lo, zero, xp)], axis=0)
        ys.append(_dot(jnp.concatenate(ms, axis=1), xcat))
    y = jnp.concatenate(ys, axis=1)

    xsw = xs * expand(c_dtw)

    def new_state(hg, col_row, upd):
        parts = [hg[h * pd:(h + 1) * pd, :] * col_row[:, c_ecl + h:c_ecl + h + 1] for h in range(nh)]
        return (jnp.concatenate(parts, axis=0) + upd).reshape(nh, pd, SSD_STATE)

    if nseq == 1:
        hg = h_ref[0, gh].reshape(gw, SSD_STATE)
        yoff = _dot_nt(cmb, hg.astype(BF16))
        h_ref[0, gh] = new_state(hg, colf[0:1, :], _dot_tn(xsw.astype(BF16), bmb))
    else:
        xsw_scr[...] = xsw
        bm_scr[...] = bm
        cm_scr[...] = cm
        col_scr[...] = colf

        def seq_step(i, carry):
            r0 = pl.multiple_of(i * r, r)
            rs = pl.ds(r0, r)
            hg = h_ref[i, gh].reshape(gw, SSD_STATE)
            yoff_scr[rs, :] = _dot_nt(cm_scr[rs, :].astype(BF16), hg.astype(BF16))
            upd = _dot_tn(xsw_scr[rs, :].astype(BF16), bm_scr[rs, :].astype(BF16))
            h_ref[i, gh] = new_state(hg, col_scr[pl.ds(r0, 1), :], upd)
            return carry

        lax.fori_loop(0, nseq, seq_step, 0, unroll=2)
        yoff = yoff_scr[...]

    y = y + yoff * expand(c_e) + dsk_ref[:, gc] * xs
    y = y * _silu(z_ref[:, gc])
    o_ref[:, gc] = (_rms(y) * ng_ref[:, gc]).astype(o_ref.dtype)


def _ssd(p_main, p_dt, nblk, nc, nseq, gps, conv0, conv_w, conv_b, dtb, alog, dskip, norm_g,
         h0_all, lin, out_prev, lout, n_out):
    g = SSD_GROUPS // gps
    gw = gps * SSD_INNER // SSD_GROUPS
    ns = gps * SSD_STATE
    nh = gps * SSD_HPG
    tri, seqm = _block_masks(nseq)
    h_in, h_out, h_shape, x_specs, x_args = _stacked_state_io(
        h0_all, lin, out_prev, lout, n_out, (nseq, nh, SSD_HEADDIM, SSD_STATE), lambda b, gg, c: (b, gg, 0, 0))
    xo = P_XBC
    bo = P_XBC + SSD_INNER
    co = bo + SSD_GROUPS * SSD_STATE

    def cols(width, off):
        return lambda b, gg, c: (0, off // width + gg)

    in_specs = [
        pl.BlockSpec((ROWS, gw), lambda b, gg, c: (b * nc + c, P_Z // gw + gg)),
        pl.BlockSpec((ROWS, gw), lambda b, gg, c: (b * nc + c, xo // gw + gg)),
        pl.BlockSpec((ROWS, ns), lambda b, gg, c: (b * nc + c, bo // ns + gg)),
        pl.BlockSpec((ROWS, ns), lambda b, gg, c: (b * nc + c, co // ns + gg)),
        pl.BlockSpec((ROWS, ns), lambda b, gg, c: (b * nc + c, gg)),
        pl.BlockSpec((nseq, 8, gw), lambda b, gg, c: (b, 0, gg)),
        pl.BlockSpec((nseq, 8, ns), lambda b, gg, c: (b, 0, (bo - xo) // ns + gg)),
        pl.BlockSpec((nseq, 8, ns), lambda b, gg, c: (b, 0, (co - xo) // ns + gg)),
        pl.BlockSpec((SSD_CONV, gw), cols(gw, 0)),
        pl.BlockSpec((SSD_CONV, ns), cols(ns, bo - xo)),
        pl.BlockSpec((SSD_CONV, ns), cols(ns, co - xo)),
        pl.BlockSpec((1, gw), cols(gw, 0)),
        pl.BlockSpec((1, ns), cols(ns, bo - xo)),
        pl.BlockSpec((1, ns), cols(ns, co - xo)),
        pl.BlockSpec((nh, ROWS), lambda b, gg, c: (gg, 0)),
        pl.BlockSpec((nh, ROWS), lambda b, gg, c: (gg, 0)),
        pl.BlockSpec((1, gw), cols(gw, 0)),
        pl.BlockSpec((1, gw), cols(gw, 0)),
        pl.BlockSpec((ROWS, ROWS), lambda b, gg, c: (0, 0)),
        pl.BlockSpec((ROWS, ROWS), lambda b, gg, c: (0, 0)),
        pl.BlockSpec((ROWS, ROWS), lambda b, gg, c: (0, 0)),
        h_in,
    ] + x_specs
    out_specs = [pl.BlockSpec((ROWS, gw), lambda b, gg, c: (b * nc + c, gg)), h_out]
    out_shape = [jax.ShapeDtypeStruct((nblk * nc * ROWS, SSD_INNER), BF16), h_shape]
    assert nseq == 1 or gps == 1
    gw1 = SSD_INNER // SSD_GROUPS
    scratch = [pltpu.VMEM((8, gw), F32), pltpu.VMEM((8, ns), F32), pltpu.VMEM((8, ns), F32),
               pltpu.VMEM((ROWS, gw1), F32), pltpu.VMEM((ROWS, SSD_STATE), F32), pltpu.VMEM((ROWS, SSD_STATE), F32),
               pltpu.VMEM((ROWS, 128), F32), pltpu.VMEM((ROWS, gw1), F32)]
    return pl.pallas_call(
        functools.partial(_ssd_body, nseq=nseq, gps=gps),
        grid=(nblk, g, nc),
        in_specs=in_specs, out_specs=out_specs, out_shape=out_shape,
        scratch_shapes=scratch,
        input_output_aliases={len(in_specs) - 1: 1} if x_args else {},
        compiler_params=_cparams(("parallel", "parallel", "arbitrary")),
        name="ssd",
    )(p_main, p_main, p_main, p_main, p_dt, conv0, conv0, conv0, conv_w, conv_w, conv_w,
      conv_b, conv_b, conv_b, dtb, alog, dskip, norm_g, tri, tri.T, seqm, h0_all, *x_args)


def _head_rows(v):
    return jnp.broadcast_to(v[:, None], (SSD_HEADS, ROWS))


def _prep_w_in(w_in):
    w_main = jnp.concatenate([w_in[:, :W_U], w_in[:, W_GATE:], w_in[:, W_Z:W_DT]], axis=1).astype(BF16)
    w_u = w_in[:, W_U:W_Z].astype(BF16)
    w_dt = w_in[:, W_DT:W_GATE].reshape(-1, SSD_GROUPS, SSD_HPG)
    w_dt = jnp.pad(w_dt, ((0, 0), (0, 0), (0, 128 - SSD_HPG))).reshape(-1, SSD_GROUPS * 128).astype(BF16)
    return w_main, w_u, w_dt


def _pad_conv(c):
    return jnp.pad(c, ((0, 0), (8 - (SSD_CONV - 1), 0), (0, 0)))


class _Stream:
    def __init__(self, b, l, pos0, s5_tl, s5_bb, ret_hps, ssd_gps):
        self.b, self.l = b, l
        self.nseq = 1 if l % ROWS == 0 else ROWS // l
        self.nc = max(l // ROWS, 1)
        self.nblk = b // self.nseq
        self.s5_tl, self.s5_bb, self.ret_hps, self.ssd_gps = s5_tl, s5_bb, ret_hps, ssd_gps
        pos = pos0 + jnp.arange(self.nc * ROWS, dtype=jnp.int32) % l
        self.cos, self.sin = _rotary_tables(pos)
        self.rtab = _retention_tables(self.nseq)


def _layer(x, st, w, depth, lout, ret_all, ssm_all, lin, ret_prev, ssm_prev, s5r0, s5i0, conv0):
    b, l = st.b, st.l
    whole_seq = st.nseq == 1
    x, h_mix = _ffn(x, w["ffn1_norm"], w["ffn1_wg"], w["ffn1_wu"], w["ffn1_wd"], w["mix_norm"], BF16)
    p_main = _matmul(h_mix, w["w_main"])
    p_dt = _matmul(h_mix, w["w_dt"])

    o_ret, ret_out = _retention(p_main, st.nblk, st.nc, st.nseq, st.ret_hps, st.cos, st.sin, st.rtab,
                                w["ret_ln_g"], ret_all, lin, ret_prev, lout, depth)

    if whole_seq:
        u_tm = _matmul(h_mix, w["w_u"], seq_len=l)
    else:
        u_tm = _to_time_major(_matmul(h_mix, w["w_u"]), b, l, st.s5_bb)
    y5, s5r, s5i = _s5(u_tm, w["s5_wts"], w["s5_abr"], w["s5_abi"], w["s5_d"], s5r0, s5i0, st.s5_tl, st.s5_bb)
    s5_pre = y5 if whole_seq else _from_time_major(y5, b, l, st.s5_bb)

    o_ssd, ssm_out = _ssd(p_main, p_dt, st.nblk, st.nc, st.nseq, st.ssd_gps, conv0, w["conv_w"], w["conv_b"],
                          w["dtb"], w["alog"], w["dskip"], w["ssd_norm"], ssm_all, lin, ssm_prev, lout, depth)
    conv_new = p_main.reshape(b, l, -1)[:, l - (SSD_CONV - 1):, P_XBC:P_XBC + SSD_CONV_DIM]

    x = _merge(x, o_ret, s5_pre, o_ssd, p_main, w["ret_wo"], w["s5_wglu"], w["ssd_wo"], w["w_out"],
               s5_seq_len=l if whole_seq else None)
    x, y_fin = _ffn(x, w["ffn2_norm"], w["ffn2_wg"], w["ffn2_wu"], w["ffn2_wd"],
                    w["final_norm"] if lout == depth - 1 else None, F32)
    s5_new = jnp.stack([s5r, s5i], axis=-1).reshape(b, S5_GROUPS, S5_STATE, 2)
    return x, y_fin, ret_out, ssm_out, s5_new, conv_new


def kernel(x_prompt, x_sample, state_ret, state_s5, state_ssm, state_conv, ffn1_norm, ffn1_w_gu, ffn1_w_down, mix_norm, w_in, ret_ln_g, ret_w_o, s5_a_re, s5_a_im, s5_log_dt, s5_b_re, s5_b_im, s5_c_re, s5_c_im, s5_d, s5_w_glu, ssd_conv_w, ssd_conv_b, ssd_dt_bias, ssd_a_log, ssd_d, ssd_norm, ssd_w_o, w_out, ffn2_norm, ffn2_w_gu, ffn2_w_down, final_norm):
    depth = w_in.shape[0]
    bp, lp, d = x_prompt.shape
    bs, ls, _ = x_sample.shape
    assert lp % ROWS == 0 and ROWS % ls == 0 and bs % (ROWS // ls) == 0 and bp % 8 == 0 and bs % 32 == 0
    st_p = _Stream(bp, lp, 0, s5_tl=64, s5_bb=bp, ret_hps=RET_HEADS, ssd_gps=SSD_GROUPS)
    st_s = _Stream(bs, ls, PAST_LEN, s5_tl=ls, s5_bb=32, ret_hps=1, ssd_gps=1)

    xp = x_prompt.reshape(bp * lp, d)
    xs = x_sample.reshape(bs * ls, d)
    zero_ret = jnp.zeros((1, bp, RET_HEADS, RET_DK, RET_DV), F32)
    zero_s5 = jnp.zeros((bp, S5_GROUPS * S5_STATE), F32)
    zero_ssm = jnp.zeros((1, bp, SSD_HEADS, SSD_HEADDIM, SSD_STATE), F32)
    zero_conv = jnp.zeros((bp, 8, SSD_CONV_DIM), F32)

    small = [[] for _ in range(4)]
    yp = ys = None
    ret_p = jnp.zeros((depth,) + zero_ret.shape[1:], F32)
    ssm_p = jnp.zeros((depth,) + zero_ssm.shape[1:], F32)
    ret_s = jnp.zeros(state_ret.shape, F32)
    ssm_s = jnp.zeros(state_ssm.shape, F32)
    f = ffn1_w_gu.shape[2] // 2
    for l in range(depth):
        abr, abi, bbr, bbi = _s5_discretize(s5_a_re[l], s5_a_im[l], s5_log_dt[l], s5_b_re[l], s5_b_im[l])
        w_main, w_u, w_dt = _prep_w_in(w_in[l])
        w = dict(
            ffn1_norm=ffn1_norm[l], ffn1_wg=ffn1_w_gu[l, :, :f].astype(BF16), ffn1_wu=ffn1_w_gu[l, :, f:].astype(BF16),
            ffn1_wd=ffn1_w_down[l].astype(BF16), mix_norm=mix_norm[l], w_main=w_main, w_u=w_u, w_dt=w_dt,
            ret_ln_g=ret_ln_g[l], s5_wts=_s5_block_weights(bbr, bbi, s5_c_re[l], s5_c_im[l]), s5_abr=abr, s5_abi=abi,
            s5_d=s5_d[l], conv_w=ssd_conv_w[l], conv_b=ssd_conv_b[l].reshape(1, -1), dtb=_head_rows(ssd_dt_bias[l]),
            alog=_head_rows(ssd_a_log[l]), dskip=jnp.repeat(ssd_d[l], SSD_HEADDIM).reshape(1, SSD_INNER),
            ssd_norm=ssd_norm[l].reshape(1, SSD_INNER), ret_wo=ret_w_o[l].astype(BF16),
            s5_wglu=s5_w_glu[l].astype(BF16), ssd_wo=ssd_w_o[l].astype(BF16), w_out=w_out[l].astype(BF16),
            ffn2_norm=ffn2_norm[l], ffn2_wg=ffn2_w_gu[l, :, :f].astype(BF16), ffn2_wu=ffn2_w_gu[l, :, f:].astype(BF16),
            ffn2_wd=ffn2_w_down[l].astype(BF16), final_norm=final_norm,
        )
        xp, yp, ret_p, ssm_p, s1, c1 = _layer(xp, st_p, w, depth, l, zero_ret, zero_ssm, 0, ret_p, ssm_p,
                                              zero_s5, zero_s5, zero_conv)
        ss = state_s5[l].reshape(bs, S5_GROUPS * S5_STATE, 2)
        xs, ys, ret_s, ssm_s, s2, c2 = _layer(xs, st_s, w, depth, l, state_ret, state_ssm, l, ret_s, ssm_s,
                                              ss[..., 0], ss[..., 1], _pad_conv(state_conv[l]))
        for lst, v in zip(small, (s1, s2, c1, c2)):
            lst.append(v)

    s5_p, s5_s, conv_p, conv_s = (jnp.stack(o) for o in small)
    return (yp.reshape(bp, lp, d), ys.reshape(bs, ls, d), ret_p, ret_s, s5_p, s5_s, ssm_p, ssm_s, conv_p, conv_s)
```

```python
import functools

import numpy as np
import jax
import jax.numpy as jnp
from jax import lax
from jax.experimental import pallas as pl
from jax.experimental.pallas import tpu as pltpu

F32 = jnp.float32
BF16 = jnp.bfloat16

D_MODEL = 1024
PAST_LEN = 16384
EPS = 1e-6
RET_HEADS = 4
RET_DK = 128
RET_DV = 256
ROPE_BASE = 10000.0
S5_GROUPS = 64
S5_STATE = 64
SSD_INNER = 2048
SSD_HEADDIM = 64
SSD_HEADS = 32
SSD_GROUPS = 4
SSD_HPG = 8
SSD_STATE = 128
SSD_CONV = 4
SSD_CONV_DIM = 3072

ROWS = 128
VMEM_LIMIT = 56 * 1024 * 1024

P_Z, P_XBC, P_Q, P_K, P_V, P_G = 0, 2048, 5120, 5632, 6144, 7168
PG_GATE, PG_DT = 0, 3072
W_DT, W_GATE = 9216, 9248
W_TILE = 1024
W_MAIN_TILES = (4, 5, 6, 7, 8, 0, 1, 2)
W_U_TILES = (3,)


def _cparams(sem):
    return pltpu.CompilerParams(dimension_semantics=sem, vmem_limit_bytes=VMEM_LIMIT)


def _pick(n, cands):
    for c in cands:
        if n % c == 0:
            return c
    raise ValueError(f"no tile for {n}")


def _resident(shape):
    nd = len(shape)
    return pl.BlockSpec(shape, lambda *_: (0,) * nd, pipeline_mode=pl.Buffered(1))


def _rms(x):
    return x * lax.rsqrt(jnp.mean(x * x, axis=-1, keepdims=True) + EPS)


def _silu(x):
    return x * jax.nn.sigmoid(x)


def _softplus(x):
    return jnp.maximum(x, 0.0) + jnp.log1p(jnp.exp(-jnp.abs(x)))


def _dot(a, b):
    return jnp.dot(a, b, preferred_element_type=F32)


def _dot_nt(a, b):
    return lax.dot_general(a, b, (((1,), (1,)), ((), ())), preferred_element_type=F32)


def _dot_tn(a, b):
    return lax.dot_general(a, b, (((0,), (0,)), ((), ())), preferred_element_type=F32)


def _dot_f32(a, b):
    return jnp.dot(a, b, preferred_element_type=F32, precision=lax.Precision.HIGHEST)


def _ffn_body(x_ref, ng_ref, wg_ref, wu_ref, wd_ref, *rest, n_chunks, tf):
    o_ref = rest[-2] if len(rest) == 3 else rest[0]
    x = x_ref[...]
    hb = (_rms(x) * ng_ref[...]).astype(BF16)
    acc = jnp.zeros(x.shape, F32)
    for c in range(n_chunks):
        sl = slice(c * tf, (c + 1) * tf)
        g = _dot(hb, wg_ref[:, sl])
        u = _dot(hb, wu_ref[:, sl])
        acc = acc + _dot((_silu(g) * u).astype(BF16), wd_ref[sl, :])
    y = x + 0.5 * acc
    o_ref[...] = y
    if len(rest) == 3:
        pg_ref, _, h_ref = rest
        h_ref[...] = (_rms(y) * pg_ref[...]).astype(h_ref.dtype)


def _ffn(x, norm_g, w_g, w_u, w_d, post_g=None, post_dtype=None):
    t, d = x.shape
    f = w_g.shape[1]
    tm = _pick(t, (512, 256, 128))
    n_chunks = 2
    tf = f // n_chunks
    row = pl.BlockSpec((1, d), lambda i: (0, 0))
    tile = pl.BlockSpec((tm, d), lambda i: (i, 0))
    with_post = post_g is not None
    res = pl.pallas_call(
        functools.partial(_ffn_body, n_chunks=n_chunks, tf=tf),
        grid=(t // tm,),
        in_specs=[tile, row, _resident((d, f)), _resident((d, f)), _resident((f, d))] + [row] * with_post,
        out_specs=[tile] + [tile] * with_post,
        out_shape=[jax.ShapeDtypeStruct((t, d), F32)] + [jax.ShapeDtypeStruct((t, d), post_dtype)] * with_post,
        compiler_params=_cparams(("parallel",)),
        name="ffn",
    )(x, norm_g.reshape(1, d), w_g, w_u, w_d, *([post_g.reshape(1, d)] if with_post else []))
    return res if with_post else (res[0], None)


def _mm_body(a_ref, w_ref, o_ref):
    o_ref[...] = _dot(a_ref[...], w_ref[...])


def _matmul(a, w, seq_len=None, layer=None, tiles=None):
    t, k = a.shape
    tm = _pick(seq_len or t, (2048, 1024, 512, 256, 128))
    if tiles is None:
        n = w.shape[1]
        tn = _pick(n, (1024, 512, 256, 128))
        w_spec = pl.BlockSpec((k, tn), lambda i, j: (0, j))
    else:
        tn = W_TILE
        n = len(tiles) * tn

        def src_tile(j):
            r = tiles[0]
            for dst, src in enumerate(tiles[1:], 1):
                r = jnp.where(j == dst, src, r)
            return r

        w_spec = pl.BlockSpec((None, k, tn), lambda i, j: (layer, 0, src_tile(j)))
    if seq_len is None:
        out_spec = pl.BlockSpec((tm, tn), lambda i, j: (i, j))
        out_shape = (t, n)
    else:
        assert tn == n
        nt = seq_len // tm
        out_spec = pl.BlockSpec((tm, n), lambda i, j: (i % nt, i // nt))
        out_shape = (seq_len, (t // seq_len) * n)
    return pl.pallas_call(
        _mm_body,
        grid=(t // tm, n // tn),
        in_specs=[pl.BlockSpec((tm, k), lambda i, j: (i, 0)), w_spec],
        out_specs=out_spec,
        out_shape=jax.ShapeDtypeStruct(out_shape, F32),
        compiler_params=_cparams(("parallel", "parallel")),
        name="in_proj",
    )(a, w)


def _merge_body(x_ref, oret_ref, s5_ref, ossd_ref, gl_ref, wro_ref, wglu_ref, wso_ref, wout_ref, o_ref):
    d = D_MODEL
    y_ret = _dot(oret_ref[...], wro_ref[...])
    yag = _dot(s5_ref[...], wglu_ref[...])
    y_s5 = yag[:, :d] * jax.nn.sigmoid(yag[:, d:])
    y_ssd = _dot(ossd_ref[...], wso_ref[...])
    gl = gl_ref[...]
    merged = (jax.nn.sigmoid(gl[:, :d]) * y_ret + jax.nn.sigmoid(gl[:, d:2 * d]) * y_s5
              + jax.nn.sigmoid(gl[:, 2 * d:]) * y_ssd)
    o_ref[...] = x_ref[...] + _dot(merged.astype(BF16), wout_ref[...])


def _merge(x, o_ret, s5_pre, o_ssd, p_main, w_ro, w_glu, w_so, w_out, s5_seq_len=None):
    t, d = x.shape
    tm = _pick(s5_seq_len or t, (512, 256, 128))
    if s5_seq_len is None:
        s5_spec = pl.BlockSpec((tm, d), lambda i: (i, 0))
    else:
        nt = s5_seq_len // tm
        s5_spec = pl.BlockSpec((tm, d), lambda i: (i % nt, i // nt))
    return pl.pallas_call(
        _merge_body,
        grid=(t // tm,),
        in_specs=[
            pl.BlockSpec((tm, d), lambda i: (i, 0)),
            pl.BlockSpec((tm, d), lambda i: (i, 0)),
            s5_spec,
            pl.BlockSpec((tm, 2 * d), lambda i: (i, 0)),
            pl.BlockSpec((tm, 3 * d), lambda i: (i, PG_GATE // (3 * d))),
            _resident((d, d)), _resident((d, 2 * d)), _resident((2 * d, d)), _resident((d, d)),
        ],
        out_specs=pl.BlockSpec((tm, d), lambda i: (i, 0)),
        out_shape=jax.ShapeDtypeStruct((t, d), F32),
        compiler_params=_cparams(("parallel",)),
        name="merge",
    )(x, o_ret, s5_pre, o_ssd, p_main, w_ro, w_glu, w_so, w_out)


def _seq_index(nseq):
    r = ROWS // nseq
    i = np.arange(ROWS)
    return i // r, i % r, r


def _block_masks(nseq):
    s, t, _ = _seq_index(nseq)
    same = s[:, None] == s[None, :]
    causal = same & (t[None, :] <= t[:, None])
    return jnp.asarray(causal, F32), jnp.asarray(same, F32)


def _retention_tables(nseq):
    s, t, r = _seq_index(nseq)
    lg = jnp.log1p(-jnp.exp2(-5.0 - jnp.arange(RET_HEADS, dtype=F32)))[:, None, None]
    tf = jnp.asarray(t, F32)
    diff = tf[:, None] - tf[None, :]
    causal = jnp.asarray((s[:, None] == s[None, :]) & (t[None, :] <= t[:, None]))
    decay = jnp.where(causal[None], jnp.exp(jnp.where(causal, diff, 0.0)[None] * lg), 0.0)
    qdec = jnp.broadcast_to(jnp.exp((tf + 1.0)[None, :, None] * lg), (RET_HEADS, ROWS, RET_DV))
    kdec = jnp.broadcast_to(jnp.exp((r - 1.0 - tf)[None, :, None] * lg), (RET_HEADS, ROWS, RET_DK))
    cdec = jnp.broadcast_to(jnp.exp(r * lg), (RET_HEADS, RET_DK, RET_DV))
    return decay, qdec, kdec, cdec


def _rotary_tables(pos):
    half = RET_DK // 2
    inv = ROPE_BASE ** (-jnp.arange(half, dtype=F32) / half)
    ang = pos.astype(F32)[:, None] * inv[None, :]
    cos, sin = jnp.cos(ang), jnp.sin(ang)
    return jnp.concatenate([cos, cos], axis=1), jnp.concatenate([-sin, sin], axis=1)


def _ret_body(q_ref, k_ref, v_ref, g_ref, cos_ref, sin_ref, dec_ref, qd_ref, kd_ref, cd_ref, lng_ref,
              s0_ref, *rest, nseq, hps):
    o_ref, s_ref, q_scr, kd_scr, cross_scr = rest[-5:]
    r = ROWS // nseq

    @pl.when(pl.program_id(2) == 0)
    def _():
        s_ref[...] = s0_ref[...]

    cos = cos_ref[...]
    sin = sin_ref[...]

    def rot(x):
        return x * cos + pltpu.roll(x, RET_DK // 2, 1) * sin

    for h in range(hps):
        ks = slice(h * RET_DK, (h + 1) * RET_DK)
        vs = slice(h * RET_DV, (h + 1) * RET_DV)
        q = rot(q_ref[:, ks])
        k = rot(k_ref[:, ks]) * (RET_DK ** -0.5)
        qb = q.astype(BF16)
        vb = v_ref[:, vs].astype(BF16)
        scores = _dot_nt(qb, k.astype(BF16)) * dec_ref[h]
        inner = _dot(scores.astype(BF16), vb)
        kd = k * kd_ref[h]
        cd = cd_ref[h]
        if nseq == 1:
            s = s_ref[0, h]
            cross = _dot(qb, s.astype(BF16))
            s_ref[0, h] = s * cd + _dot_tn(kd.astype(BF16), vb)
        else:
            q_scr[...] = q
            kd_scr[...] = kd

            def seq_step(i, carry, h=h, vs=vs, cd=cd):
                rs = pl.ds(pl.multiple_of(i * r, r), r)
                s = s_ref[i, h]
                cross_scr[rs, :] = _dot(q_scr[rs, :].astype(BF16), s.astype(BF16))
                s_ref[i, h] = s * cd + _dot_tn(kd_scr[rs, :].astype(BF16), v_ref[rs, vs].astype(BF16))
                return carry

            lax.fori_loop(0, nseq, seq_step, 0, unroll=4)
            cross = cross_scr[...]
        o = inner + cross * qd_ref[h]
        mu = jnp.mean(o, axis=-1, keepdims=True)
        oc = o - mu
        var = jnp.mean(oc * oc, axis=-1, keepdims=True)
        on = (oc * lax.rsqrt(var + EPS)) * lng_ref[:, vs]
        o_ref[:, vs] = (_silu(g_ref[:, vs]) * on).astype(o_ref.dtype)


def _stacked_state_io(state_all, lin, out_prev, lout, n_out, blk, idx):
    in_spec = pl.BlockSpec((None,) + blk, lambda *g: (lin,) + idx(*g))
    out_spec = pl.BlockSpec((None,) + blk, lambda *g: (lout,) + idx(*g))
    out_shape = jax.ShapeDtypeStruct((n_out,) + state_all.shape[1:], state_all.dtype)
    assert out_prev.shape == out_shape.shape
    return in_spec, out_spec, out_shape, [pl.BlockSpec(memory_space=pl.ANY)], [out_prev]


def _retention(p_main, nblk, nc, nseq, hps, cos, sin, tabs, ln_g, s0_all, lin, out_prev, lout, n_out):
    h = RET_HEADS
    hb = h // hps
    decay, qdec, kdec, cdec = tabs
    kw, vw = hps * RET_DK, hps * RET_DV
    s_in, s_out, s_shape, x_specs, x_args = _stacked_state_io(
        s0_all, lin, out_prev, lout, n_out, (nseq, hps, RET_DK, RET_DV), lambda b, hh, c: (b, hh, 0, 0))

    in_specs = [
        pl.BlockSpec((ROWS, kw), lambda b, hh, c: (b * nc + c, P_Q // kw + hh)),
        pl.BlockSpec((ROWS, kw), lambda b, hh, c: (b * nc + c, P_K // kw + hh)),
        pl.BlockSpec((ROWS, vw), lambda b, hh, c: (b * nc + c, P_V // vw + hh)),
        pl.BlockSpec((ROWS, vw), lambda b, hh, c: (b * nc + c, P_G // vw + hh)),
        pl.BlockSpec((ROWS, RET_DK), lambda b, hh, c: (c, 0)),
        pl.BlockSpec((ROWS, RET_DK), lambda b, hh, c: (c, 0)),
        pl.BlockSpec((hps, ROWS, ROWS), lambda b, hh, c: (hh, 0, 0)),
        pl.BlockSpec((hps, ROWS, RET_DV), lambda b, hh, c: (hh, 0, 0)),
        pl.BlockSpec((hps, ROWS, RET_DK), lambda b, hh, c: (hh, 0, 0)),
        pl.BlockSpec((hps, RET_DK, RET_DV), lambda b, hh, c: (hh, 0, 0)),
        pl.BlockSpec((1, vw), lambda b, hh, c: (0, hh)),
        s_in,
    ] + x_specs
    out_specs = [pl.BlockSpec((ROWS, vw), lambda b, hh, c: (b * nc + c, hh)), s_out]
    out_shape = [jax.ShapeDtypeStruct((nblk * nc * ROWS, h * RET_DV), BF16), s_shape]
    return pl.pallas_call(
        functools.partial(_ret_body, nseq=nseq, hps=hps),
        grid=(nblk, hb, nc),
        in_specs=in_specs, out_specs=out_specs, out_shape=out_shape,
        scratch_shapes=[pltpu.VMEM((ROWS, RET_DK), F32), pltpu.VMEM((ROWS, RET_DK), F32),
                        pltpu.VMEM((ROWS, RET_DV), F32)],
        input_output_aliases={len(in_specs) - 1: 1} if x_args else {},
        compiler_params=_cparams(("parallel", "parallel", "arbitrary")),
        name="retention",
    )(p_main, p_main, p_main, p_main, cos, sin, decay, qdec, kdec, cdec, ln_g.reshape(1, -1), s0_all, *x_args)


def _s5_disc_body(ar_ref, ai_ref, ldt_ref, br_ref, bi_ref, abr_ref, abi_ref, bbr_ref, bbi_ref):
    ar = ar_ref[...]
    ai = ai_ref[...]
    dt = jnp.exp(ldt_ref[...])
    mag = jnp.exp(dt * ar)
    abr = mag * jnp.cos(dt * ai)
    abi = mag * jnp.sin(dt * ai)
    nr = abr - 1.0
    ni = abi
    den = ar * ar + ai * ai
    f_re = (nr * ar + ni * ai) / den
    f_im = (ni * ar - nr * ai) / den
    br = br_ref[...]
    bi = bi_ref[...]
    abr_ref[...] = abr
    abi_ref[...] = abi
    bbr_ref[...] = f_re * br - f_im * bi
    bbi_ref[...] = f_re * bi + f_im * br


def _s5_discretize(a_re, a_im, log_dt, b_re, b_im):
    g, n, c = b_re.shape
    flat = (g * c * n // 128, 128)

    def rep(a):
        return jnp.broadcast_to(a[:, None, :], (g, c, n)).reshape(flat)

    args = (rep(a_re), rep(a_im), rep(jnp.broadcast_to(log_dt[:, None], (g, n))),
            jnp.swapaxes(b_re, 1, 2).reshape(flat), jnp.swapaxes(b_im, 1, 2).reshape(flat))
    spec = pl.BlockSpec(flat, lambda: (0, 0))
    abr, abi, bbr, bbi = pl.pallas_call(
        _s5_disc_body,
        in_specs=[spec] * 5, out_specs=[spec] * 4,
        out_shape=[jax.ShapeDtypeStruct(flat, F32)] * 4,
        name="s5_discretize",
    )(*args)
    abr = abr.reshape(g, c, n)[:, 0, :].reshape(1, g * n)
    abi = abi.reshape(g, c, n)[:, 0, :].reshape(1, g * n)
    return abr, abi, bbr.reshape(g, c, n), bbi.reshape(g, c, n)


def _s5_block_weights(bb_re, bb_im, c_re, c_im):
    g, c, n = bb_re.shape
    ng = 16
    eye = jnp.eye(ng, dtype=F32)

    def b_blk(bb):
        x = bb.reshape(g // ng, ng, c, n)
        return jnp.einsum("kgcn,gh->kgchn", x, eye).reshape(g // ng, ng * c, ng * n).astype(BF16)

    def c_blk(cc):
        x = cc.reshape(g // ng, ng, c, n)
        return jnp.einsum("kgcn,gh->kgnhc", x, eye).reshape(g // ng, ng * n, ng * c).astype(BF16)

    return b_blk(bb_re), b_blk(bb_im), c_blk(c_re), c_blk(c_im)


def _s5_body(u_ref, bre_ref, bim_ref, cre_ref, cim_ref, ar_ref, ai_ref, d_ref, s0r_ref, s0i_ref,
             y_ref, sr_ref, si_ref, bur, bui, *, tl, bb, wide):
    n_cg = bre_ref.shape[0]
    cw = bre_ref.shape[1]
    sw = bre_ref.shape[2]
    w = n_cg * cw
    lc = 512

    @pl.when(pl.program_id(1) == 0)
    def _():
        sr_ref[...] = s0r_ref[...]
        si_ref[...] = s0i_ref[...]

    if wide:
        ux = u_ref[...]
        u = jnp.swapaxes(jnp.stack([ux[:, b * w:(b + 1) * w] for b in range(bb)], axis=0), 0, 1)
        u = u.reshape(tl * bb, w)
    else:
        u = u_ref[...]
    ub = u.astype(BF16)
    for cg in range(n_cg):
        ucg = ub[:, cg * cw:(cg + 1) * cw]
        bur[:, cg * sw:(cg + 1) * sw] = _dot(ucg, bre_ref[cg])
        bui[:, cg * sw:(cg + 1) * sw] = _dot(ucg, bim_ref[cg])

    for bt in range(bb // 8):
        rs = slice(bt * 8, (bt + 1) * 8)
        for ci in range(n_cg * sw // lc):
            cs = slice(ci * lc, (ci + 1) * lc)
            ar = jnp.broadcast_to(ar_ref[:, cs], (8, lc))
            ai = jnp.broadcast_to(ai_ref[:, cs], (8, lc))

            def step(l, carry, cs=cs, ar=ar, ai=ai, bt=bt):
                xr, xi = carry
                r0 = pl.multiple_of(l * bb + bt * 8, 8)
                nr = (ar * xr - ai * xi) + bur[pl.ds(r0, 8), cs]
                ni = (ar * xi + ai * xr) + bui[pl.ds(r0, 8), cs]
                bur[pl.ds(r0, 8), cs] = nr
                bui[pl.ds(r0, 8), cs] = ni
                return nr, ni

            xr, xi = lax.fori_loop(0, tl, step, (sr_ref[rs, cs], si_ref[rs, cs]), unroll=2)
            sr_ref[rs, cs] = xr
            si_ref[rs, cs] = xi

    for cg in range(n_cg):
        xs = slice(cg * sw, (cg + 1) * sw)
        us = slice(cg * cw, (cg + 1) * cw)
        y = _dot(bur[:, xs].astype(BF16), cre_ref[cg]) - _dot(bui[:, xs].astype(BF16), cim_ref[cg])
        y = jax.nn.gelu(y + d_ref[:, us] * u[:, us])
        if wide:
            y = jnp.swapaxes(y.reshape(tl, bb, cw), 0, 1)
            for b in range(bb):
                y_ref[:, b * w + cg * cw:b * w + (cg + 1) * cw] = y[b].astype(y_ref.dtype)
        else:
            y_ref[:, us] = y.astype(y_ref.dtype)


def _s5(u_tm, wts, abr, abi, d, s0r, s0i, tl, bb):
    b, ns = s0r.shape
    bre, bim, cre, cim = wts
    w = bre.shape[0] * bre.shape[1]
    wide = u_tm.shape[1] != w
    rows = tl * bb
    nbb = b // bb
    if wide:
        assert bb == b and u_tm.shape[1] == b * w
        ntb = u_tm.shape[0] // tl
        io_spec = pl.BlockSpec((tl, b * w), lambda i, j: (j, 0))
    else:
        ntb = u_tm.shape[0] // (rows * nbb)
        io_spec = pl.BlockSpec((rows, w), lambda i, j: (i * ntb + j, 0))
    return pl.pallas_call(
        functools.partial(_s5_body, tl=tl, bb=bb, wide=wide),
        grid=(nbb, ntb),
        in_specs=[
            io_spec,
            _resident(bre.shape), _resident(bim.shape), _resident(cre.shape), _resident(cim.shape),
            _resident((1, ns)), _resident((1, ns)), _resident((1, w)),
            pl.BlockSpec((bb, ns), lambda i, j: (i, 0)),
            pl.BlockSpec((bb, ns), lambda i, j: (i, 0)),
        ],
        out_specs=[
            io_spec,
            pl.BlockSpec((bb, ns), lambda i, j: (i, 0)),
            pl.BlockSpec((bb, ns), lambda i, j: (i, 0)),
        ],
        out_shape=[
            jax.ShapeDtypeStruct(u_tm.shape, BF16),
            jax.ShapeDtypeStruct((b, ns), F32),
            jax.ShapeDtypeStruct((b, ns), F32),
        ],
        scratch_shapes=[pltpu.VMEM((rows, ns), F32), pltpu.VMEM((rows, ns), F32)],
        compiler_params=_cparams(("parallel", "arbitrary")),
        name="s5",
    )(u_tm, bre, bim, cre, cim, abr, abi, d.reshape(1, w), s0r, s0i)


def _to_time_major(u, b, l, bb):
    w = u.shape[-1]
    return u.reshape(b // bb, bb, l, w).transpose(0, 2, 1, 3).reshape(b * l, w)


def _from_time_major(y, b, l, bb):
    w = y.shape[-1]
    return y.reshape(b // bb, l, bb, w).transpose(0, 2, 1, 3).reshape(b * l, w)


def _ssd_body(z_ref, xs_ref, bm_ref, cm_ref, dt_ref, px_ref, pb_ref, pc_ref, wx_ref, wb_ref, wc_ref,
              bx_ref, bb_ref, bc_ref, dtb_ref, alog_ref, dsk_ref, ng_ref, tri_ref, trit_ref, seqm_ref, h0_ref,
              *rest, nseq, gps):
    o_ref, h_ref, tx, tb, tc, xsw_scr, bm_scr, cm_scr, col_scr, yoff_scr = rest[-10:]

    @pl.when(pl.program_id(2) == 0)
    def _():
        h_ref[...] = h0_ref[...]
        if nseq == 1:
            tx[...] = px_ref[0]
            tb[...] = pb_ref[0]
            tc[...] = pc_ref[0]

    for gi in range(gps):
        _ssd_group(gi, z_ref, xs_ref, bm_ref, cm_ref, dt_ref, px_ref, pb_ref, pc_ref, wx_ref, wb_ref, wc_ref,
                   bx_ref, bb_ref, bc_ref, dtb_ref, alog_ref, dsk_ref, ng_ref, tri_ref, trit_ref, seqm_ref,
                   o_ref, h_ref, tx, tb, tc, xsw_scr, bm_scr, cm_scr, col_scr, yoff_scr, nseq)


def _ssd_group(gi, z_ref, xs_ref, bm_ref, cm_ref, dt_ref, px_ref, pb_ref, pc_ref, wx_ref, wb_ref, wc_ref,
               bx_ref, bb_ref, bc_ref, dtb_ref, alog_ref, dsk_ref, ng_ref, tri_ref, trit_ref, seqm_ref,
               o_ref, h_ref, tx, tb, tc, xsw_scr, bm_scr, cm_scr, col_scr, yoff_scr, nseq):
    r = ROWS // nseq
    pd = SSD_HEADDIM
    nh = SSD_HPG
    gw = nh * pd
    gc = slice(gi * gw, (gi + 1) * gw)
    gn = slice(gi * SSD_STATE, (gi + 1) * SSD_STATE)
    gh = slice(gi * nh, (gi + 1) * nh)
    row = lax.broadcasted_iota(jnp.int32, (ROWS, 1), 0)
    t_idx = row & (r - 1)
    lo = lax.broadcasted_iota(jnp.int32, (1, ROWS), 1) < pd

    def conv(x_ref, tail, prev_ref, w_ref, b_ref, cs):
        x = x_ref[:, cs]
        acc = b_ref[:, cs] + w_ref[SSD_CONV - 1:SSD_CONV, cs] * x
        if nseq == 1:
            xe = jnp.concatenate([tail[:, cs], x], axis=0)
            tail[:, cs] = x[ROWS - 8:, :]
        else:
            z8 = prev_ref[:, :, cs].reshape(ROWS, x.shape[1])
        for k in range(1, SSD_CONV):
            if nseq == 1:
                xk = xe[8 - k:8 - k + ROWS, :]
            else:
                xk = jnp.where(t_idx >= k, pltpu.roll(x, k, 0), pltpu.roll(z8, ROWS - 8 + k, 0))
            acc = acc + w_ref[SSD_CONV - 1 - k:SSD_CONV - k, cs] * xk
        return _silu(acc)

    xs = conv(xs_ref, tx, px_ref, wx_ref, bx_ref, gc)
    bm = conv(bm_ref, tb, pb_ref, wb_ref, bb_ref, gn)
    cm = conv(cm_ref, tc, pc_ref, wc_ref, bc_ref, gn)
    bmb = bm.astype(BF16)
    cmb = cm.astype(BF16)
    xsb = xs.astype(BF16)

    dt8 = _softplus(dt_ref[:, gn].T[0:nh, :] + dtb_ref[gh, :])
    dta8 = dt8 * (-jnp.exp(alog_ref[gh, :]))
    cum8 = _dot_f32(dta8, trit_ref[...])
    clast8 = _dot_f32(dta8, seqm_ref[...])
    e8 = jnp.exp(cum8)
    dtw8 = dt8 * jnp.exp(clast8 - cum8)
    ecl8 = jnp.exp(clast8)
    colf = jnp.concatenate([cum8, e8, dtw8, ecl8, jnp.zeros((ROWS - 4 * nh, ROWS), F32)], axis=0).T
    c_cum, c_e, c_dtw, c_ecl = 0, nh, 2 * nh, 3 * nh
    causal = tri_ref[...] > 0.0
    cb = _dot_nt(cmb, bmb)

    def expand(off):
        return jnp.concatenate(
            [jnp.where(lo, colf[:, off + 2 * p:off + 2 * p + 1], colf[:, off + 2 * p + 1:off + 2 * p + 2])
             for p in range(nh // 2)], axis=1)

    ys = []
    zero = jnp.zeros((), BF16)
    for p in range(nh // 2):
        ms = []
        for h in (2 * p, 2 * p + 1):
            seg = colf[:, c_cum + h:c_cum + h + 1] - cum8[h:h + 1, :]
            lm = jnp.where(causal, jnp.exp(seg), 0.0)
            ms.append(((cb * lm) * dt8[h:h + 1, :]).astype(BF16))
        xp = xsb[:, p * 2 * pd:(p + 1) * 2 * pd]
        xcat = jnp.concatenate([jnp.where(lo, xp, zero), jnp.where(lo, zero, xp)], axis=0)
        ys.append(_dot(jnp.concatenate(ms, axis=1), xcat))
    y = jnp.concatenate(ys, axis=1)

    xsw = xs * expand(c_dtw)

    def new_state(hg, col_row, upd):
        parts = [hg[h * pd:(h + 1) * pd, :] * col_row[:, c_ecl + h:c_ecl + h + 1] for h in range(nh)]
        return (jnp.concatenate(parts, axis=0) + upd).reshape(nh, pd, SSD_STATE)

    if nseq == 1:
        hg = h_ref[0, gh].reshape(gw, SSD_STATE)
        yoff = _dot_nt(cmb, hg.astype(BF16))
        h_ref[0, gh] = new_state(hg, colf[0:1, :], _dot_tn(xsw.astype(BF16), bmb))
    else:
        xsw_scr[...] = xsw
        bm_scr[...] = bm
        cm_scr[...] = cm
        col_scr[...] = colf

        def seq_step(i, carry):
            r0 = pl.multiple_of(i * r, r)
            rs = pl.ds(r0, r)
            hg = h_ref[i, gh].reshape(gw, SSD_STATE)
            yoff_scr[rs, :] = _dot_nt(cm_scr[rs, :].astype(BF16), hg.astype(BF16))
            upd = _dot_tn(xsw_scr[rs, :].astype(BF16), bm_scr[rs, :].astype(BF16))
            h_ref[i, gh] = new_state(hg, col_scr[pl.ds(r0, 1), :], upd)
            return carry

        lax.fori_loop(0, nseq, seq_step, 0, unroll=2)
        yoff = yoff_scr[...]

    y = y + yoff * expand(c_e) + dsk_ref[:, gc] * xs
    y = y * _silu(z_ref[:, gc])
    o_ref[:, gc] = (_rms(y) * ng_ref[:, gc]).astype(o_ref.dtype)


def _ssd(p_main, p_dt, nblk, nc, nseq, gps, conv0, conv_w, conv_b, dtb, alog, dskip, norm_g,
         h0_all, lin, out_prev, lout, n_out):
    g = SSD_GROUPS // gps
    gw = gps * SSD_INNER // SSD_GROUPS
    ns = gps * SSD_STATE
    nh = gps * SSD_HPG
    tri, seqm = _block_masks(nseq)
    h_in, h_out, h_shape, x_specs, x_args = _stacked_state_io(
        h0_all, lin, out_prev, lout, n_out, (nseq, nh, SSD_HEADDIM, SSD_STATE), lambda b, gg, c: (b, gg, 0, 0))
    xo = P_XBC
    bo = P_XBC + SSD_INNER
    co = bo + SSD_GROUPS * SSD_STATE

    def cols(width, off):
        return lambda b, gg, c: (0, off // width + gg)

    in_specs = [
        pl.BlockSpec((ROWS, gw), lambda b, gg, c: (b * nc + c, P_Z // gw + gg)),
        pl.BlockSpec((ROWS, gw), lambda b, gg, c: (b * nc + c, xo // gw + gg)),
        pl.BlockSpec((ROWS, ns), lambda b, gg, c: (b * nc + c, bo // ns + gg)),
        pl.BlockSpec((ROWS, ns), lambda b, gg, c: (b * nc + c, co // ns + gg)),
        pl.BlockSpec((ROWS, ns), lambda b, gg, c: (b * nc + c, PG_DT // ns + gg)),
        pl.BlockSpec((nseq, 8, gw), lambda b, gg, c: (b, 0, gg)),
        pl.BlockSpec((nseq, 8, ns), lambda b, gg, c: (b, 0, (bo - xo) // ns + gg)),
        pl.BlockSpec((nseq, 8, ns), lambda b, gg, c: (b, 0, (co - xo) // ns + gg)),
        pl.BlockSpec((SSD_CONV, gw), cols(gw, 0)),
        pl.BlockSpec((SSD_CONV, ns), cols(ns, bo - xo)),
        pl.BlockSpec((SSD_CONV, ns), cols(ns, co - xo)),
        pl.BlockSpec((1, gw), cols(gw, 0)),
        pl.BlockSpec((1, ns), cols(ns, bo - xo)),
        pl.BlockSpec((1, ns), cols(ns, co - xo)),
        pl.BlockSpec((nh, ROWS), lambda b, gg, c: (gg, 0)),
        pl.BlockSpec((nh, ROWS), lambda b, gg, c: (gg, 0)),
        pl.BlockSpec((1, gw), cols(gw, 0)),
        pl.BlockSpec((1, gw), cols(gw, 0)),
        pl.BlockSpec((ROWS, ROWS), lambda b, gg, c: (0, 0)),
        pl.BlockSpec((ROWS, ROWS), lambda b, gg, c: (0, 0)),
        pl.BlockSpec((ROWS, ROWS), lambda b, gg, c: (0, 0)),
        h_in,
    ] + x_specs
    out_specs = [pl.BlockSpec((ROWS, gw), lambda b, gg, c: (b * nc + c, gg)), h_out]
    out_shape = [jax.ShapeDtypeStruct((nblk * nc * ROWS, SSD_INNER), BF16), h_shape]
    assert nseq == 1 or gps == 1
    gw1 = SSD_INNER // SSD_GROUPS
    scratch = [pltpu.VMEM((8, gw), F32), pltpu.VMEM((8, ns), F32), pltpu.VMEM((8, ns), F32),
               pltpu.VMEM((ROWS, gw1), F32), pltpu.VMEM((ROWS, SSD_STATE), F32), pltpu.VMEM((ROWS, SSD_STATE), F32),
               pltpu.VMEM((ROWS, 128), F32), pltpu.VMEM((ROWS, gw1), F32)]
    return pl.pallas_call(
        functools.partial(_ssd_body, nseq=nseq, gps=gps),
        grid=(nblk, g, nc),
        in_specs=in_specs, out_specs=out_specs, out_shape=out_shape,
        scratch_shapes=scratch,
        input_output_aliases={len(in_specs) - 1: 1} if x_args else {},
        compiler_params=_cparams(("parallel", "parallel", "arbitrary")),
        name="ssd",
    )(p_main, p_main, p_main, p_main, p_dt, conv0, conv0, conv0, conv_w, conv_w, conv_w,
      conv_b, conv_b, conv_b, dtb, alog, dskip, norm_g, tri, tri.T, seqm, h0_all, *x_args)


def _head_rows(v):
    return jnp.broadcast_to(v[:, None], (SSD_HEADS, ROWS))


def _gate_dt_weight(w_in_l):
    w_dt = w_in_l[:, W_DT:W_GATE].reshape(-1, SSD_GROUPS, SSD_HPG)
    w_dt = jnp.pad(w_dt, ((0, 0), (0, 0), (0, 128 - SSD_HPG))).reshape(-1, SSD_GROUPS * 128)
    return jnp.concatenate([w_in_l[:, W_GATE:], w_dt], axis=1)


def _pad_conv(c):
    return jnp.pad(c, ((0, 0), (8 - (SSD_CONV - 1), 0), (0, 0)))


class _Stream:
    def __init__(self, b, l, pos0, s5_tl, s5_bb, ret_hps, ssd_gps):
        self.b, self.l = b, l
        self.nseq = 1 if l % ROWS == 0 else ROWS // l
        self.nc = max(l // ROWS, 1)
        self.nblk = b // self.nseq
        self.s5_tl, self.s5_bb, self.ret_hps, self.ssd_gps = s5_tl, s5_bb, ret_hps, ssd_gps
        pos = pos0 + jnp.arange(self.nc * ROWS, dtype=jnp.int32) % l
        self.cos, self.sin = _rotary_tables(pos)
        self.rtab = _retention_tables(self.nseq)


def _layer(x, st, w, depth, lout, ret_all, ssm_all, lin, ret_prev, ssm_prev, s5r0, s5i0, conv0):
    b, l = st.b, st.l
    whole_seq = st.nseq == 1
    x, h_mix = _ffn(x, w["ffn1_norm"], w["ffn1_wg"], w["ffn1_wu"], w["ffn1_wd"], w["mix_norm"], BF16)
    p_main = _matmul(h_mix, w["w_in"], layer=lout, tiles=W_MAIN_TILES)
    p_gd = _matmul(h_mix, w["w_gd"])

    o_ret, ret_out = _retention(p_main, st.nblk, st.nc, st.nseq, st.ret_hps, st.cos, st.sin, st.rtab,
                                w["ret_ln_g"], ret_all, lin, ret_prev, lout, depth)

    if whole_seq:
        u_tm = _matmul(h_mix, w["w_in"], seq_len=l, layer=lout, tiles=W_U_TILES)
    else:
        u_tm = _to_time_major(_matmul(h_mix, w["w_in"], layer=lout, tiles=W_U_TILES), b, l, st.s5_bb)
    y5, s5r, s5i = _s5(u_tm, w["s5_wts"], w["s5_abr"], w["s5_abi"], w["s5_d"], s5r0, s5i0, st.s5_tl, st.s5_bb)
    s5_pre = y5 if whole_seq else _from_time_major(y5, b, l, st.s5_bb)

    o_ssd, ssm_out = _ssd(p_main, p_gd, st.nblk, st.nc, st.nseq, st.ssd_gps, conv0, w["conv_w"], w["conv_b"],
                          w["dtb"], w["alog"], w["dskip"], w["ssd_norm"], ssm_all, lin, ssm_prev, lout, depth)
    conv_new = p_main.reshape(b, l, -1)[:, l - (SSD_CONV - 1):, P_XBC:P_XBC + SSD_CONV_DIM]

    x = _merge(x, o_ret, s5_pre, o_ssd, p_gd, w["ret_wo"], w["s5_wglu"], w["ssd_wo"], w["w_out"],
               s5_seq_len=l if whole_seq else None)
    x, y_fin = _ffn(x, w["ffn2_norm"], w["ffn2_wg"], w["ffn2_wu"], w["ffn2_wd"],
                    w["final_norm"] if lout == depth - 1 else None, F32)
    s5_new = jnp.stack([s5r, s5i], axis=-1).reshape(b, S5_GROUPS, S5_STATE, 2)
    return x, y_fin, ret_out, ssm_out, s5_new, conv_new


def kernel(x_prompt, x_sample, state_ret, state_s5, state_ssm, state_conv, ffn1_norm, ffn1_w_gu, ffn1_w_down, mix_norm, w_in, ret_ln_g, ret_w_o, s5_a_re, s5_a_im, s5_log_dt, s5_b_re, s5_b_im, s5_c_re, s5_c_im, s5_d, s5_w_glu, ssd_conv_w, ssd_conv_b, ssd_dt_bias, ssd_a_log, ssd_d, ssd_norm, ssd_w_o, w_out, ffn2_norm, ffn2_w_gu, ffn2_w_down, final_norm):
    depth = w_in.shape[0]
    bp, lp, d = x_prompt.shape
    bs, ls, _ = x_sample.shape
    assert lp % ROWS == 0 and ROWS % ls == 0 and bs % (ROWS // ls) == 0 and bp % 8 == 0 and bs % 32 == 0
    st_p = _Stream(bp, lp, 0, s5_tl=64, s5_bb=bp, ret_hps=RET_HEADS, ssd_gps=SSD_GROUPS)
    st_s = _Stream(bs, ls, PAST_LEN, s5_tl=ls, s5_bb=32, ret_hps=2, ssd_gps=1)

    xp = x_prompt.reshape(bp * lp, d)
    xs = x_sample.reshape(bs * ls, d)
    zero_ret = jnp.zeros((1, bp, RET_HEADS, RET_DK, RET_DV), F32)
    zero_s5 = jnp.zeros((bp, S5_GROUPS * S5_STATE), F32)
    zero_ssm = jnp.zeros((1, bp, SSD_HEADS, SSD_HEADDIM, SSD_STATE), F32)
    zero_conv = jnp.zeros((bp, 8, SSD_CONV_DIM), F32)

    small = [[] for _ in range(4)]
    yp = ys = None
    ret_p = jnp.zeros((depth,) + zero_ret.shape[1:], F32)
    ssm_p = jnp.zeros((depth,) + zero_ssm.shape[1:], F32)
    ret_s = jnp.zeros(state_ret.shape, F32)
    ssm_s = jnp.zeros(state_ssm.shape, F32)
    f = ffn1_w_gu.shape[2] // 2
    w_in_b = w_in.astype(BF16)
    for l in range(depth):
        abr, abi, bbr, bbi = _s5_discretize(s5_a_re[l], s5_a_im[l], s5_log_dt[l], s5_b_re[l], s5_b_im[l])
        w = dict(
            ffn1_norm=ffn1_norm[l], ffn1_wg=ffn1_w_gu[l, :, :f].astype(BF16), ffn1_wu=ffn1_w_gu[l, :, f:].astype(BF16),
            ffn1_wd=ffn1_w_down[l].astype(BF16), mix_norm=mix_norm[l], w_in=w_in_b, w_gd=_gate_dt_weight(w_in_b[l]),
            ret_ln_g=ret_ln_g[l], s5_wts=_s5_block_weights(bbr, bbi, s5_c_re[l], s5_c_im[l]), s5_abr=abr, s5_abi=abi,
            s5_d=s5_d[l], conv_w=ssd_conv_w[l], conv_b=ssd_conv_b[l].reshape(1, -1), dtb=_head_rows(ssd_dt_bias[l]),
            alog=_head_rows(ssd_a_log[l]), dskip=jnp.repeat(ssd_d[l], SSD_HEADDIM).reshape(1, SSD_INNER),
            ssd_norm=ssd_norm[l].reshape(1, SSD_INNER), ret_wo=ret_w_o[l].astype(BF16),
            s5_wglu=s5_w_glu[l].astype(BF16), ssd_wo=ssd_w_o[l].astype(BF16), w_out=w_out[l].astype(BF16),
            ffn2_norm=ffn2_norm[l], ffn2_wg=ffn2_w_gu[l, :, :f].astype(BF16), ffn2_wu=ffn2_w_gu[l, :, f:].astype(BF16),
            ffn2_wd=ffn2_w_down[l].astype(BF16), final_norm=final_norm,
        )
        xp, yp, ret_p, ssm_p, s1, c1 = _layer(xp, st_p, w, depth, l, zero_ret, zero_ssm, 0, ret_p, ssm_p,
                                              zero_s5, zero_s5, zero_conv)
        ss = state_s5[l].reshape(bs, S5_GROUPS * S5_STATE, 2)
        xs, ys, ret_s, ssm_s, s2, c2 = _layer(xs, st_s, w, depth, l, state_ret, state_ssm, l, ret_s, ssm_s,
                                              ss[..., 0], ss[..., 1], _pad_conv(state_conv[l]))
        for lst, v in zip(small, (s1, s2, c1, c2)):
            lst.append(v)

    s5_p, s5_s, conv_p, conv_s = (jnp.stack(o) for o in small)
    return (yp.reshape(bp, lp, d), ys.reshape(bs, ls, d), ret_p, ret_s, s5_p, s5_s, ssm_p, ssm_s, conv_p, conv_s)
```

```python
import functools

import numpy as np
import jax
import jax.numpy as jnp
from jax import lax
from jax.experimental import pallas as pl
from jax.experimental.pallas import tpu as pltpu

F32 = jnp.float32
BF16 = jnp.bfloat16

D_MODEL = 1024
PAST_LEN = 16384
EPS = 1e-6
RET_HEADS = 4
RET_DK = 128
RET_DV = 256
ROPE_BASE = 10000.0
S5_GROUPS = 64
S5_STATE = 64
SSD_INNER = 2048
SSD_HEADDIM = 64
SSD_HEADS = 32
SSD_GROUPS = 4
SSD_HPG = 8
SSD_STATE = 128
SSD_CONV = 4
SSD_CONV_DIM = 3072

ROWS = 128
VMEM_LIMIT = 56 * 1024 * 1024

P_Z, P_XBC, P_Q, P_K, P_V, P_G = 0, 2048, 5120, 5632, 6144, 7168
PG_GATE, PG_DT, PG_WIDTH = 0, 3072, 3584
W_DT, W_GATE = 9216, 9248
W_TILE = 1024
W_MAIN_TILES = (4, 5, 6, 7, 8, 0, 1, 2)
W_U_TILES = (3,)


def _cparams(sem):
    return pltpu.CompilerParams(dimension_semantics=sem, vmem_limit_bytes=VMEM_LIMIT)


def _pick(n, cands):
    for c in cands:
        if n % c == 0:
            return c
    raise ValueError(f"no tile for {n}")


def _resident(shape):
    nd = len(shape)
    return pl.BlockSpec(shape, lambda *_: (0,) * nd, pipeline_mode=pl.Buffered(1))


def _rms(x):
    return x * lax.rsqrt(jnp.mean(x * x, axis=-1, keepdims=True) + EPS)


def _silu(x):
    return x * jax.nn.sigmoid(x)


def _softplus(x):
    return jnp.maximum(x, 0.0) + jnp.log1p(jnp.exp(-jnp.abs(x)))


def _dot(a, b):
    return jnp.dot(a, b, preferred_element_type=F32)


def _dot_nt(a, b):
    return lax.dot_general(a, b, (((1,), (1,)), ((), ())), preferred_element_type=F32)


def _dot_tn(a, b):
    return lax.dot_general(a, b, (((0,), (0,)), ((), ())), preferred_element_type=F32)


def _dot_f32(a, b):
    return jnp.dot(a, b, preferred_element_type=F32, precision=lax.Precision.HIGHEST)


def _ffn_body(x_ref, ng_ref, wg_ref, wu_ref, wd_ref, *rest, n_chunks, tf):
    o_ref = rest[-2] if len(rest) == 3 else rest[0]
    x = x_ref[...]
    hb = (_rms(x) * ng_ref[...]).astype(BF16)
    acc = jnp.zeros(x.shape, F32)
    for c in range(n_chunks):
        sl = slice(c * tf, (c + 1) * tf)
        g = _dot(hb, wg_ref[:, sl])
        u = _dot(hb, wu_ref[:, sl])
        acc = acc + _dot((_silu(g) * u).astype(BF16), wd_ref[sl, :])
    y = x + 0.5 * acc
    o_ref[...] = y
    if len(rest) == 3:
        pg_ref, _, h_ref = rest
        h_ref[...] = (_rms(y) * pg_ref[...]).astype(h_ref.dtype)


def _ffn(x, norm_g, w_g, w_u, w_d, post_g=None, post_dtype=None):
    t, d = x.shape
    f = w_g.shape[1]
    tm = _pick(t, (512, 256, 128))
    n_chunks = 2
    tf = f // n_chunks
    row = pl.BlockSpec((1, d), lambda i: (0, 0))
    tile = pl.BlockSpec((tm, d), lambda i: (i, 0))
    with_post = post_g is not None
    res = pl.pallas_call(
        functools.partial(_ffn_body, n_chunks=n_chunks, tf=tf),
        grid=(t // tm,),
        in_specs=[tile, row, _resident((d, f)), _resident((d, f)), _resident((f, d))] + [row] * with_post,
        out_specs=[tile] + [tile] * with_post,
        out_shape=[jax.ShapeDtypeStruct((t, d), F32)] + [jax.ShapeDtypeStruct((t, d), post_dtype)] * with_post,
        compiler_params=_cparams(("parallel",)),
        name="ffn",
    )(x, norm_g.reshape(1, d), w_g, w_u, w_d, *([post_g.reshape(1, d)] if with_post else []))
    return res if with_post else (res[0], None)


def _mm_body(a_ref, w_ref, o_ref):
    o_ref[...] = _dot(a_ref[...], w_ref[...])


def _matmul(a, w, layer, tiles, tile_w=W_TILE, seq_len=None):
    t, k = a.shape
    tm = _pick(seq_len or t, (2048, 1024, 512, 256, 128))
    tn = tile_w
    n = len(tiles) * tn

    def src_tile(j):
        r = tiles[0]
        for dst, src in enumerate(tiles[1:], 1):
            r = jnp.where(j == dst, src, r)
        return r

    w_spec = pl.BlockSpec((None, k, tn), lambda i, j: (layer, 0, src_tile(j)))
    if seq_len is None:
        out_spec = pl.BlockSpec((tm, tn), lambda i, j: (i, j))
        out_shape = (t, n)
    else:
        assert tn == n
        nt = seq_len // tm
        out_spec = pl.BlockSpec((tm, n), lambda i, j: (i % nt, i // nt))
        out_shape = (seq_len, (t // seq_len) * n)
    return pl.pallas_call(
        _mm_body,
        grid=(t // tm, n // tn),
        in_specs=[pl.BlockSpec((tm, k), lambda i, j: (i, 0)), w_spec],
        out_specs=out_spec,
        out_shape=jax.ShapeDtypeStruct(out_shape, F32),
        compiler_params=_cparams(("parallel", "parallel")),
        name="in_proj",
    )(a, w)


def _merge_body(x_ref, oret_ref, s5_ref, ossd_ref, gl_ref, wro_ref, wglu_ref, wso_ref, wout_ref, o_ref):
    d = D_MODEL
    y_ret = _dot(oret_ref[...], wro_ref[...])
    yag = _dot(s5_ref[...], wglu_ref[...])
    y_s5 = yag[:, :d] * jax.nn.sigmoid(yag[:, d:])
    y_ssd = _dot(ossd_ref[...], wso_ref[...])
    gl = gl_ref[...]
    merged = (jax.nn.sigmoid(gl[:, :d]) * y_ret + jax.nn.sigmoid(gl[:, d:2 * d]) * y_s5
              + jax.nn.sigmoid(gl[:, 2 * d:]) * y_ssd)
    o_ref[...] = x_ref[...] + _dot(merged.astype(BF16), wout_ref[...])


def _merge(x, o_ret, s5_pre, o_ssd, p_main, w_ro, w_glu, w_so, w_out, s5_seq_len=None):
    t, d = x.shape
    tm = _pick(s5_seq_len or t, (512, 256, 128))
    if s5_seq_len is None:
        s5_spec = pl.BlockSpec((tm, d), lambda i: (i, 0))
    else:
        nt = s5_seq_len // tm
        s5_spec = pl.BlockSpec((tm, d), lambda i: (i % nt, i // nt))
    return pl.pallas_call(
        _merge_body,
        grid=(t // tm,),
        in_specs=[
            pl.BlockSpec((tm, d), lambda i: (i, 0)),
            pl.BlockSpec((tm, d), lambda i: (i, 0)),
            s5_spec,
            pl.BlockSpec((tm, 2 * d), lambda i: (i, 0)),
            pl.BlockSpec((tm, 3 * d), lambda i: (i, PG_GATE // (3 * d))),
            _resident((d, d)), _resident((d, 2 * d)), _resident((2 * d, d)), _resident((d, d)),
        ],
        out_specs=pl.BlockSpec((tm, d), lambda i: (i, 0)),
        out_shape=jax.ShapeDtypeStruct((t, d), F32),
        compiler_params=_cparams(("parallel",)),
        name="merge",
    )(x, o_ret, s5_pre, o_ssd, p_main, w_ro, w_glu, w_so, w_out)


def _seq_index(nseq):
    r = ROWS // nseq
    i = np.arange(ROWS)
    return i // r, i % r, r


def _block_masks(nseq):
    s, t, _ = _seq_index(nseq)
    same = s[:, None] == s[None, :]
    causal = same & (t[None, :] <= t[:, None])
    return jnp.asarray(causal, F32), jnp.asarray(same, F32)


def _retention_tables(nseq):
    s, t, r = _seq_index(nseq)
    lg = jnp.log1p(-jnp.exp2(-5.0 - jnp.arange(RET_HEADS, dtype=F32)))[:, None, None]
    tf = jnp.asarray(t, F32)
    diff = tf[:, None] - tf[None, :]
    causal = jnp.asarray((s[:, None] == s[None, :]) & (t[None, :] <= t[:, None]))
    decay = jnp.where(causal[None], jnp.exp(jnp.where(causal, diff, 0.0)[None] * lg), 0.0)
    qdec = jnp.broadcast_to(jnp.exp((tf + 1.0)[None, :, None] * lg), (RET_HEADS, ROWS, RET_DV))
    kdec = jnp.broadcast_to(jnp.exp((r - 1.0 - tf)[None, :, None] * lg), (RET_HEADS, ROWS, RET_DK))
    cdec = jnp.broadcast_to(jnp.exp(r * lg), (RET_HEADS, RET_DK, RET_DV))
    return decay, qdec, kdec, cdec


def _rotary_tables(pos):
    half = RET_DK // 2
    inv = ROPE_BASE ** (-jnp.arange(half, dtype=F32) / half)
    ang = pos.astype(F32)[:, None] * inv[None, :]
    cos, sin = jnp.cos(ang), jnp.sin(ang)
    return jnp.concatenate([cos, cos], axis=1), jnp.concatenate([-sin, sin], axis=1)


def _ret_body(q_ref, k_ref, v_ref, g_ref, cos_ref, sin_ref, dec_ref, qd_ref, kd_ref, cd_ref, lng_ref,
              s0_ref, *rest, nseq, hps):
    o_ref, s_ref, q_scr, kd_scr, cross_scr = rest[-5:]
    r = ROWS // nseq

    @pl.when(pl.program_id(2) == 0)
    def _():
        s_ref[...] = s0_ref[...]

    cos = cos_ref[...]
    sin = sin_ref[...]

    def rot(x):
        return x * cos + pltpu.roll(x, RET_DK // 2, 1) * sin

    for h in range(hps):
        ks = slice(h * RET_DK, (h + 1) * RET_DK)
        vs = slice(h * RET_DV, (h + 1) * RET_DV)
        q = rot(q_ref[:, ks])
        k = rot(k_ref[:, ks]) * (RET_DK ** -0.5)
        qb = q.astype(BF16)
        vb = v_ref[:, vs].astype(BF16)
        scores = _dot_nt(qb, k.astype(BF16)) * dec_ref[h]
        inner = _dot(scores.astype(BF16), vb)
        kd = k * kd_ref[h]
        cd = cd_ref[h]
        if nseq == 1:
            s = s_ref[0, h]
            cross = _dot(qb, s.astype(BF16))
            s_ref[0, h] = s * cd + _dot_tn(kd.astype(BF16), vb)
        else:
            q_scr[...] = q
            kd_scr[...] = kd

            def seq_step(i, carry, h=h, vs=vs, cd=cd):
                rs = pl.ds(pl.multiple_of(i * r, r), r)
                s = s_ref[i, h]
                cross_scr[rs, :] = _dot(q_scr[rs, :].astype(BF16), s.astype(BF16))
                s_ref[i, h] = s * cd + _dot_tn(kd_scr[rs, :].astype(BF16), v_ref[rs, vs].astype(BF16))
                return carry

            lax.fori_loop(0, nseq, seq_step, 0, unroll=4)
            cross = cross_scr[...]
        o = inner + cross * qd_ref[h]
        mu = jnp.mean(o, axis=-1, keepdims=True)
        oc = o - mu
        var = jnp.mean(oc * oc, axis=-1, keepdims=True)
        on = (oc * lax.rsqrt(var + EPS)) * lng_ref[:, vs]
        o_ref[:, vs] = (_silu(g_ref[:, vs]) * on).astype(o_ref.dtype)


def _stacked_state_io(state_all, lin, out_prev, lout, n_out, blk, idx):
    in_spec = pl.BlockSpec((None,) + blk, lambda *g: (lin,) + idx(*g))
    out_spec = pl.BlockSpec((None,) + blk, lambda *g: (lout,) + idx(*g))
    out_shape = jax.ShapeDtypeStruct((n_out,) + state_all.shape[1:], state_all.dtype)
    assert out_prev.shape == out_shape.shape
    return in_spec, out_spec, out_shape, [pl.BlockSpec(memory_space=pl.ANY)], [out_prev]


def _retention(p_main, nblk, nc, nseq, hps, cos, sin, tabs, ln_g, s0_all, lin, out_prev, lout, n_out):
    h = RET_HEADS
    hb = h // hps
    decay, qdec, kdec, cdec = tabs
    kw, vw = hps * RET_DK, hps * RET_DV
    s_in, s_out, s_shape, x_specs, x_args = _stacked_state_io(
        s0_all, lin, out_prev, lout, n_out, (nseq, hps, RET_DK, RET_DV), lambda b, hh, c: (b, hh, 0, 0))

    in_specs = [
        pl.BlockSpec((ROWS, kw), lambda b, hh, c: (b * nc + c, P_Q // kw + hh)),
        pl.BlockSpec((ROWS, kw), lambda b, hh, c: (b * nc + c, P_K // kw + hh)),
        pl.BlockSpec((ROWS, vw), lambda b, hh, c: (b * nc + c, P_V // vw + hh)),
        pl.BlockSpec((ROWS, vw), lambda b, hh, c: (b * nc + c, P_G // vw + hh)),
        pl.BlockSpec((ROWS, RET_DK), lambda b, hh, c: (c, 0)),
        pl.BlockSpec((ROWS, RET_DK), lambda b, hh, c: (c, 0)),
        pl.BlockSpec((hps, ROWS, ROWS), lambda b, hh, c: (hh, 0, 0)),
        pl.BlockSpec((hps, ROWS, RET_DV), lambda b, hh, c: (hh, 0, 0)),
        pl.BlockSpec((hps, ROWS, RET_DK), lambda b, hh, c: (hh, 0, 0)),
        pl.BlockSpec((hps, RET_DK, RET_DV), lambda b, hh, c: (hh, 0, 0)),
        pl.BlockSpec((1, vw), lambda b, hh, c: (0, hh)),
        s_in,
    ] + x_specs
    out_specs = [pl.BlockSpec((ROWS, vw), lambda b, hh, c: (b * nc + c, hh)), s_out]
    out_shape = [jax.ShapeDtypeStruct((nblk * nc * ROWS, h * RET_DV), BF16), s_shape]
    return pl.pallas_call(
        functools.partial(_ret_body, nseq=nseq, hps=hps),
        grid=(nblk, hb, nc),
        in_specs=in_specs, out_specs=out_specs, out_shape=out_shape,
        scratch_shapes=[pltpu.VMEM((ROWS, RET_DK), F32), pltpu.VMEM((ROWS, RET_DK), F32),
                        pltpu.VMEM((ROWS, RET_DV), F32)],
        input_output_aliases={len(in_specs) - 1: 1} if x_args else {},
        compiler_params=_cparams(("parallel", "parallel", "arbitrary")),
        name="retention",
    )(p_main, p_main, p_main, p_main, cos, sin, decay, qdec, kdec, cdec, ln_g.reshape(1, -1), s0_all, *x_args)


def _s5_disc_body(ar_ref, ai_ref, ldt_ref, br_ref, bi_ref, abr_ref, abi_ref, bbr_ref, bbi_ref):
    ar = ar_ref[...]
    ai = ai_ref[...]
    dt = jnp.exp(ldt_ref[...])
    mag = jnp.exp(dt * ar)
    abr = mag * jnp.cos(dt * ai)
    abi = mag * jnp.sin(dt * ai)
    nr = abr - 1.0
    ni = abi
    den = ar * ar + ai * ai
    f_re = (nr * ar + ni * ai) / den
    f_im = (ni * ar - nr * ai) / den
    br = br_ref[...]
    bi = bi_ref[...]
    abr_ref[...] = abr
    abi_ref[...] = abi
    bbr_ref[...] = f_re * br - f_im * bi
    bbi_ref[...] = f_re * bi + f_im * br


def _s5_discretize(a_re, a_im, log_dt, b_re, b_im):
    g, n, c = b_re.shape
    flat = (g * c * n // 128, 128)

    def rep(a):
        return jnp.broadcast_to(a[:, None, :], (g, c, n)).reshape(flat)

    args = (rep(a_re), rep(a_im), rep(jnp.broadcast_to(log_dt[:, None], (g, n))),
            jnp.swapaxes(b_re, 1, 2).reshape(flat), jnp.swapaxes(b_im, 1, 2).reshape(flat))
    spec = pl.BlockSpec(flat, lambda: (0, 0))
    abr, abi, bbr, bbi = pl.pallas_call(
        _s5_disc_body,
        in_specs=[spec] * 5, out_specs=[spec] * 4,
        out_shape=[jax.ShapeDtypeStruct(flat, F32)] * 4,
        name="s5_discretize",
    )(*args)
    abr = abr.reshape(g, c, n)[:, 0, :].reshape(1, g * n)
    abi = abi.reshape(g, c, n)[:, 0, :].reshape(1, g * n)
    return abr, abi, bbr.reshape(g, c, n), bbi.reshape(g, c, n)


def _s5_block_weights(bb_re, bb_im, c_re, c_im):
    g, c, n = bb_re.shape
    ng = 16
    eye = jnp.eye(ng, dtype=F32)

    def b_blk(bb):
        x = bb.reshape(g // ng, ng, c, n)
        return jnp.einsum("kgcn,gh->kgchn", x, eye).reshape(g // ng, ng * c, ng * n).astype(BF16)

    def c_blk(cc):
        x = cc.reshape(g // ng, ng, c, n)
        return jnp.einsum("kgcn,gh->kgnhc", x, eye).reshape(g // ng, ng * n, ng * c).astype(BF16)

    return b_blk(bb_re), b_blk(bb_im), c_blk(c_re), c_blk(c_im)


def _s5_body(u_ref, bre_ref, bim_ref, cre_ref, cim_ref, ar_ref, ai_ref, d_ref, s0r_ref, s0i_ref,
             y_ref, sr_ref, si_ref, bur, bui, *, tl, bb, wide):
    n_cg = bre_ref.shape[0]
    cw = bre_ref.shape[1]
    sw = bre_ref.shape[2]
    w = n_cg * cw
    lc = 512

    @pl.when(pl.program_id(1) == 0)
    def _():
        sr_ref[...] = s0r_ref[...]
        si_ref[...] = s0i_ref[...]

    if wide:
        ux = u_ref[...]
        u = jnp.swapaxes(jnp.stack([ux[:, b * w:(b + 1) * w] for b in range(bb)], axis=0), 0, 1)
        u = u.reshape(tl * bb, w)
    else:
        u = u_ref[...]
    ub = u.astype(BF16)
    for cg in range(n_cg):
        ucg = ub[:, cg * cw:(cg + 1) * cw]
        bur[:, cg * sw:(cg + 1) * sw] = _dot(ucg, bre_ref[cg])
        bui[:, cg * sw:(cg + 1) * sw] = _dot(ucg, bim_ref[cg])

    for bt in range(bb // 8):
        rs = slice(bt * 8, (bt + 1) * 8)
        for ci in range(n_cg * sw // lc):
            cs = slice(ci * lc, (ci + 1) * lc)
            ar = jnp.broadcast_to(ar_ref[:, cs], (8, lc))
            ai = jnp.broadcast_to(ai_ref[:, cs], (8, lc))

            def step(l, carry, cs=cs, ar=ar, ai=ai, bt=bt):
                xr, xi = carry
                r0 = pl.multiple_of(l * bb + bt * 8, 8)
                nr = (ar * xr - ai * xi) + bur[pl.ds(r0, 8), cs]
                ni = (ar * xi + ai * xr) + bui[pl.ds(r0, 8), cs]
                bur[pl.ds(r0, 8), cs] = nr
                bui[pl.ds(r0, 8), cs] = ni
                return nr, ni

            xr, xi = lax.fori_loop(0, tl, step, (sr_ref[rs, cs], si_ref[rs, cs]), unroll=2)
            sr_ref[rs, cs] = xr
            si_ref[rs, cs] = xi

    for cg in range(n_cg):
        xs = slice(cg * sw, (cg + 1) * sw)
        us = slice(cg * cw, (cg + 1) * cw)
        y = _dot(bur[:, xs].astype(BF16), cre_ref[cg]) - _dot(bui[:, xs].astype(BF16), cim_ref[cg])
        y = jax.nn.gelu(y + d_ref[:, us] * u[:, us])
        if wide:
            y = jnp.swapaxes(y.reshape(tl, bb, cw), 0, 1)
            for b in range(bb):
                y_ref[:, b * w + cg * cw:b * w + (cg + 1) * cw] = y[b].astype(y_ref.dtype)
        else:
            y_ref[:, us] = y.astype(y_ref.dtype)


def _s5(u_tm, wts, abr, abi, d, s0r, s0i, tl, bb):
    b, ns = s0r.shape
    bre, bim, cre, cim = wts
    w = bre.shape[0] * bre.shape[1]
    wide = u_tm.shape[1] != w
    rows = tl * bb
    nbb = b // bb
    if wide:
        assert bb == b and u_tm.shape[1] == b * w
        ntb = u_tm.shape[0] // tl
        io_spec = pl.BlockSpec((tl, b * w), lambda i, j: (j, 0))
    else:
        ntb = u_tm.shape[0] // (rows * nbb)
        io_spec = pl.BlockSpec((rows, w), lambda i, j: (i * ntb + j, 0))
    return pl.pallas_call(
        functools.partial(_s5_body, tl=tl, bb=bb, wide=wide),
        grid=(nbb, ntb),
        in_specs=[
            io_spec,
            _resident(bre.shape), _resident(bim.shape), _resident(cre.shape), _resident(cim.shape),
            _resident((1, ns)), _resident((1, ns)), _resident((1, w)),
            pl.BlockSpec((bb, ns), lambda i, j: (i, 0)),
            pl.BlockSpec((bb, ns), lambda i, j: (i, 0)),
        ],
        out_specs=[
            io_spec,
            pl.BlockSpec((bb, ns), lambda i, j: (i, 0)),
            pl.BlockSpec((bb, ns), lambda i, j: (i, 0)),
        ],
        out_shape=[
            jax.ShapeDtypeStruct(u_tm.shape, BF16),
            jax.ShapeDtypeStruct((b, ns), F32),
            jax.ShapeDtypeStruct((b, ns), F32),
        ],
        scratch_shapes=[pltpu.VMEM((rows, ns), F32), pltpu.VMEM((rows, ns), F32)],
        compiler_params=_cparams(("parallel", "arbitrary")),
        name="s5",
    )(u_tm, bre, bim, cre, cim, abr, abi, d.reshape(1, w), s0r, s0i)


def _to_time_major(u, b, l, bb):
    w = u.shape[-1]
    return u.reshape(b // bb, bb, l, w).transpose(0, 2, 1, 3).reshape(b * l, w)


def _from_time_major(y, b, l, bb):
    w = y.shape[-1]
    return y.reshape(b // bb, l, bb, w).transpose(0, 2, 1, 3).reshape(b * l, w)


def _ssd_body(z_ref, xs_ref, bm_ref, cm_ref, dt_ref, px_ref, pb_ref, pc_ref, wx_ref, wb_ref, wc_ref,
              bx_ref, bb_ref, bc_ref, dtb_ref, alog_ref, dsk_ref, ng_ref, tri_ref, trit_ref, seqm_ref, h0_ref,
              *rest, nseq, gps):
    o_ref, h_ref, tx, tb, tc, xsw_scr, bm_scr, cm_scr, col_scr, yoff_scr = rest[-10:]

    @pl.when(pl.program_id(2) == 0)
    def _():
        h_ref[...] = h0_ref[...]
        if nseq == 1:
            tx[...] = px_ref[0]
            tb[...] = pb_ref[0]
            tc[...] = pc_ref[0]

    for gi in range(gps):
        _ssd_group(gi, z_ref, xs_ref, bm_ref, cm_ref, dt_ref, px_ref, pb_ref, pc_ref, wx_ref, wb_ref, wc_ref,
                   bx_ref, bb_ref, bc_ref, dtb_ref, alog_ref, dsk_ref, ng_ref, tri_ref, trit_ref, seqm_ref,
                   o_ref, h_ref, tx, tb, tc, xsw_scr, bm_scr, cm_scr, col_scr, yoff_scr, nseq)


def _ssd_group(gi, z_ref, xs_ref, bm_ref, cm_ref, dt_ref, px_ref, pb_ref, pc_ref, wx_ref, wb_ref, wc_ref,
               bx_ref, bb_ref, bc_ref, dtb_ref, alog_ref, dsk_ref, ng_ref, tri_ref, trit_ref, seqm_ref,
               o_ref, h_ref, tx, tb, tc, xsw_scr, bm_scr, cm_scr, col_scr, yoff_scr, nseq):
    r = ROWS // nseq
    pd = SSD_HEADDIM
    nh = SSD_HPG
    gw = nh * pd
    gc = slice(gi * gw, (gi + 1) * gw)
    gn = slice(gi * SSD_STATE, (gi + 1) * SSD_STATE)
    gh = slice(gi * nh, (gi + 1) * nh)
    row = lax.broadcasted_iota(jnp.int32, (ROWS, 1), 0)
    t_idx = row & (r - 1)
    lo = lax.broadcasted_iota(jnp.int32, (1, ROWS), 1) < pd

    def conv(x_ref, tail, prev_ref, w_ref, b_ref, cs):
        x = x_ref[:, cs]
        acc = b_ref[:, cs] + w_ref[SSD_CONV - 1:SSD_CONV, cs] * x
        if nseq == 1:
            xe = jnp.concatenate([tail[:, cs], x], axis=0)
            tail[:, cs] = x[ROWS - 8:, :]
        else:
            z8 = prev_ref[:, :, cs].reshape(ROWS, x.shape[1])
        for k in range(1, SSD_CONV):
            if nseq == 1:
                xk = xe[8 - k:8 - k + ROWS, :]
            else:
                xk = jnp.where(t_idx >= k, pltpu.roll(x, k, 0), pltpu.roll(z8, ROWS - 8 + k, 0))
            acc = acc + w_ref[SSD_CONV - 1 - k:SSD_CONV - k, cs] * xk
        return _silu(acc)

    xs = conv(xs_ref, tx, px_ref, wx_ref, bx_ref, gc)
    bm = conv(bm_ref, tb, pb_ref, wb_ref, bb_ref, gn)
    cm = conv(cm_ref, tc, pc_ref, wc_ref, bc_ref, gn)
    bmb = bm.astype(BF16)
    cmb = cm.astype(BF16)
    xsb = xs.astype(BF16)

    dt8 = _softplus(dt_ref[:, gn].T[0:nh, :] + dtb_ref[gh, :])
    dta8 = dt8 * (-jnp.exp(alog_ref[gh, :]))
    cum8 = _dot_f32(dta8, trit_ref[...])
    clast8 = _dot_f32(dta8, seqm_ref[...])
    e8 = jnp.exp(cum8)
    dtw8 = dt8 * jnp.exp(clast8 - cum8)
    ecl8 = jnp.exp(clast8)
    colf = jnp.concatenate([cum8, e8, dtw8, ecl8, jnp.zeros((ROWS - 4 * nh, ROWS), F32)], axis=0).T
    c_cum, c_e, c_dtw, c_ecl = 0, nh, 2 * nh, 3 * nh
    causal = tri_ref[...] > 0.0
    cb = _dot_nt(cmb, bmb)

    def expand(off):
        return jnp.concatenate(
            [jnp.where(lo, colf[:, off + 2 * p:off + 2 * p + 1], colf[:, off + 2 * p + 1:off + 2 * p + 2])
             for p in range(nh // 2)], axis=1)

    ys = []
    zero = jnp.zeros((), BF16)
    for p in range(nh // 2):
        ms = []
        for h in (2 * p, 2 * p + 1):
            seg = colf[:, c_cum + h:c_cum + h + 1] - cum8[h:h + 1, :]
            lm = jnp.where(causal, jnp.exp(seg), 0.0)
            ms.append(((cb * lm) * dt8[h:h + 1, :]).astype(BF16))
        xp = xsb[:, p * 2 * pd:(p + 1) * 2 * pd]
        xcat = jnp.concatenate([jnp.where(lo, xp, zero), jnp.where(lo, zero, xp)], axis=0)
        ys.append(_dot(jnp.concatenate(ms, axis=1), xcat))
    y = jnp.concatenate(ys, axis=1)

    xsw = xs * expand(c_dtw)

    def new_state(hg, col_row, upd):
        parts = [hg[h * pd:(h + 1) * pd, :] * col_row[:, c_ecl + h:c_ecl + h + 1] for h in range(nh)]
        return (jnp.concatenate(parts, axis=0) + upd).reshape(nh, pd, SSD_STATE)

    if nseq == 1:
        hg = h_ref[0, gh].reshape(gw, SSD_STATE)
        yoff = _dot_nt(cmb, hg.astype(BF16))
        h_ref[0, gh] = new_state(hg, colf[0:1, :], _dot_tn(xsw.astype(BF16), bmb))
    else:
        xsw_scr[...] = xsw
        bm_scr[...] = bm
        cm_scr[...] = cm
        col_scr[...] = colf

        def seq_step(i, carry):
            r0 = pl.multiple_of(i * r, r)
            rs = pl.ds(r0, r)
            hg = h_ref[i, gh].reshape(gw, SSD_STATE)
            yoff_scr[rs, :] = _dot_nt(cm_scr[rs, :].astype(BF16), hg.astype(BF16))
            upd = _dot_tn(xsw_scr[rs, :].astype(BF16), bm_scr[rs, :].astype(BF16))
            h_ref[i, gh] = new_state(hg, col_scr[pl.ds(r0, 1), :], upd)
            return carry

        lax.fori_loop(0, nseq, seq_step, 0, unroll=2)
        yoff = yoff_scr[...]

    y = y + yoff * expand(c_e) + dsk_ref[:, gc] * xs
    y = y * _silu(z_ref[:, gc])
    o_ref[:, gc] = (_rms(y) * ng_ref[:, gc]).astype(o_ref.dtype)


def _ssd(p_main, p_dt, nblk, nc, nseq, gps, conv0, conv_w, conv_b, dtb, alog, dskip, norm_g,
         h0_all, lin, out_prev, lout, n_out):
    g = SSD_GROUPS // gps
    gw = gps * SSD_INNER // SSD_GROUPS
    ns = gps * SSD_STATE
    nh = gps * SSD_HPG
    tri, seqm = _block_masks(nseq)
    h_in, h_out, h_shape, x_specs, x_args = _stacked_state_io(
        h0_all, lin, out_prev, lout, n_out, (nseq, nh, SSD_HEADDIM, SSD_STATE), lambda b, gg, c: (b, gg, 0, 0))
    xo = P_XBC
    bo = P_XBC + SSD_INNER
    co = bo + SSD_GROUPS * SSD_STATE

    def cols(width, off):
        return lambda b, gg, c: (0, off // width + gg)

    in_specs = [
        pl.BlockSpec((ROWS, gw), lambda b, gg, c: (b * nc + c, P_Z // gw + gg)),
        pl.BlockSpec((ROWS, gw), lambda b, gg, c: (b * nc + c, xo // gw + gg)),
        pl.BlockSpec((ROWS, ns), lambda b, gg, c: (b * nc + c, bo // ns + gg)),
        pl.BlockSpec((ROWS, ns), lambda b, gg, c: (b * nc + c, co // ns + gg)),
        pl.BlockSpec((ROWS, ns), lambda b, gg, c: (b * nc + c, PG_DT // ns + gg)),
        pl.BlockSpec((nseq, 8, gw), lambda b, gg, c: (b, 0, gg)),
        pl.BlockSpec((nseq, 8, ns), lambda b, gg, c: (b, 0, (bo - xo) // ns + gg)),
        pl.BlockSpec((nseq, 8, ns), lambda b, gg, c: (b, 0, (co - xo) // ns + gg)),
        pl.BlockSpec((SSD_CONV, gw), cols(gw, 0)),
        pl.BlockSpec((SSD_CONV, ns), cols(ns, bo - xo)),
        pl.BlockSpec((SSD_CONV, ns), cols(ns, co - xo)),
        pl.BlockSpec((1, gw), cols(gw, 0)),
        pl.BlockSpec((1, ns), cols(ns, bo - xo)),
        pl.BlockSpec((1, ns), cols(ns, co - xo)),
        pl.BlockSpec((nh, ROWS), lambda b, gg, c: (gg, 0)),
        pl.BlockSpec((nh, ROWS), lambda b, gg, c: (gg, 0)),
        pl.BlockSpec((1, gw), cols(gw, 0)),
        pl.BlockSpec((1, gw), cols(gw, 0)),
        pl.BlockSpec((ROWS, ROWS), lambda b, gg, c: (0, 0)),
        pl.BlockSpec((ROWS, ROWS), lambda b, gg, c: (0, 0)),
        pl.BlockSpec((ROWS, ROWS), lambda b, gg, c: (0, 0)),
        h_in,
    ] + x_specs
    out_specs = [pl.BlockSpec((ROWS, gw), lambda b, gg, c: (b * nc + c, gg)), h_out]
    out_shape = [jax.ShapeDtypeStruct((nblk * nc * ROWS, SSD_INNER), BF16), h_shape]
    assert nseq == 1 or gps == 1
    gw1 = SSD_INNER // SSD_GROUPS
    scratch = [pltpu.VMEM((8, gw), F32), pltpu.VMEM((8, ns), F32), pltpu.VMEM((8, ns), F32),
               pltpu.VMEM((ROWS, gw1), F32), pltpu.VMEM((ROWS, SSD_STATE), F32), pltpu.VMEM((ROWS, SSD_STATE), F32),
               pltpu.VMEM((ROWS, 128), F32), pltpu.VMEM((ROWS, gw1), F32)]
    return pl.pallas_call(
        functools.partial(_ssd_body, nseq=nseq, gps=gps),
        grid=(nblk, g, nc),
        in_specs=in_specs, out_specs=out_specs, out_shape=out_shape,
        scratch_shapes=scratch,
        input_output_aliases={len(in_specs) - 1: 1} if x_args else {},
        compiler_params=_cparams(("parallel", "parallel", "arbitrary")),
        name="ssd",
    )(p_main, p_main, p_main, p_main, p_dt, conv0, conv0, conv0, conv_w, conv_w, conv_w,
      conv_b, conv_b, conv_b, dtb, alog, dskip, norm_g, tri, tri.T, seqm, h0_all, *x_args)


def _head_rows(v):
    return jnp.broadcast_to(v[:, None], (SSD_HEADS, ROWS))


def _gate_dt_weight(w_in_b):
    nl, k, _ = w_in_b.shape
    w_dt = w_in_b[:, :, W_DT:W_GATE].reshape(nl, k, SSD_GROUPS, SSD_HPG)
    w_dt = jnp.pad(w_dt, ((0, 0), (0, 0), (0, 0), (0, 128 - SSD_HPG))).reshape(nl, k, SSD_GROUPS * 128)
    return jnp.concatenate([w_in_b[:, :, W_GATE:], w_dt], axis=2)


def _pad_conv(c):
    return jnp.pad(c, ((0, 0), (8 - (SSD_CONV - 1), 0), (0, 0)))


class _Stream:
    def __init__(self, b, l, pos0, s5_tl, s5_bb, ret_hps, ssd_gps):
        self.b, self.l = b, l
        self.nseq = 1 if l % ROWS == 0 else ROWS // l
        self.nc = max(l // ROWS, 1)
        self.nblk = b // self.nseq
        self.s5_tl, self.s5_bb, self.ret_hps, self.ssd_gps = s5_tl, s5_bb, ret_hps, ssd_gps
        pos = pos0 + jnp.arange(self.nc * ROWS, dtype=jnp.int32) % l
        self.cos, self.sin = _rotary_tables(pos)
        self.rtab = _retention_tables(self.nseq)


def _layer(x, st, w, depth, lout, ret_all, ssm_all, lin, ret_prev, ssm_prev, s5r0, s5i0, conv0):
    b, l = st.b, st.l
    whole_seq = st.nseq == 1
    x, h_mix = _ffn(x, w["ffn1_norm"], w["ffn1_wg"], w["ffn1_wu"], w["ffn1_wd"], w["mix_norm"], BF16)
    p_main = _matmul(h_mix, w["w_in"], layer=lout, tiles=W_MAIN_TILES)
    p_gd = _matmul(h_mix, w["w_gd"], layer=lout, tiles=(0, 1), tile_w=PG_WIDTH // 2)

    o_ret, ret_out = _retention(p_main, st.nblk, st.nc, st.nseq, st.ret_hps, st.cos, st.sin, st.rtab,
                                w["ret_ln_g"], ret_all, lin, ret_prev, lout, depth)

    if whole_seq:
        u_tm = _matmul(h_mix, w["w_in"], seq_len=l, layer=lout, tiles=W_U_TILES)
    else:
        u_tm = _to_time_major(_matmul(h_mix, w["w_in"], layer=lout, tiles=W_U_TILES), b, l, st.s5_bb)
    y5, s5r, s5i = _s5(u_tm, w["s5_wts"], w["s5_abr"], w["s5_abi"], w["s5_d"], s5r0, s5i0, st.s5_tl, st.s5_bb)
    s5_pre = y5 if whole_seq else _from_time_major(y5, b, l, st.s5_bb)

    o_ssd, ssm_out = _ssd(p_main, p_gd, st.nblk, st.nc, st.nseq, st.ssd_gps, conv0, w["conv_w"], w["conv_b"],
                          w["dtb"], w["alog"], w["dskip"], w["ssd_norm"], ssm_all, lin, ssm_prev, lout, depth)
    conv_new = p_main.reshape(b, l, -1)[:, l - (SSD_CONV - 1):, P_XBC:P_XBC + SSD_CONV_DIM]

    x = _merge(x, o_ret, s5_pre, o_ssd, p_gd, w["ret_wo"], w["s5_wglu"], w["ssd_wo"], w["w_out"],
               s5_seq_len=l if whole_seq else None)
    x, y_fin = _ffn(x, w["ffn2_norm"], w["ffn2_wg"], w["ffn2_wu"], w["ffn2_wd"],
                    w["final_norm"] if lout == depth - 1 else None, F32)
    s5_new = jnp.stack([s5r, s5i], axis=-1).reshape(b, S5_GROUPS, S5_STATE, 2)
    return x, y_fin, ret_out, ssm_out, s5_new, conv_new


def kernel(x_prompt, x_sample, state_ret, state_s5, state_ssm, state_conv, ffn1_norm, ffn1_w_gu, ffn1_w_down, mix_norm, w_in, ret_ln_g, ret_w_o, s5_a_re, s5_a_im, s5_log_dt, s5_b_re, s5_b_im, s5_c_re, s5_c_im, s5_d, s5_w_glu, ssd_conv_w, ssd_conv_b, ssd_dt_bias, ssd_a_log, ssd_d, ssd_norm, ssd_w_o, w_out, ffn2_norm, ffn2_w_gu, ffn2_w_down, final_norm):
    depth = w_in.shape[0]
    bp, lp, d = x_prompt.shape
    bs, ls, _ = x_sample.shape
    assert lp % ROWS == 0 and ROWS % ls == 0 and bs % (ROWS // ls) == 0 and bp % 8 == 0 and bs % 32 == 0
    st_p = _Stream(bp, lp, 0, s5_tl=64, s5_bb=bp, ret_hps=RET_HEADS, ssd_gps=SSD_GROUPS)
    st_s = _Stream(bs, ls, PAST_LEN, s5_tl=ls, s5_bb=32, ret_hps=2, ssd_gps=1)

    xp = x_prompt.reshape(bp * lp, d)
    xs = x_sample.reshape(bs * ls, d)
    zero_ret = jnp.zeros((1, bp, RET_HEADS, RET_DK, RET_DV), F32)
    zero_s5 = jnp.zeros((bp, S5_GROUPS * S5_STATE), F32)
    zero_ssm = jnp.zeros((1, bp, SSD_HEADS, SSD_HEADDIM, SSD_STATE), F32)
    zero_conv = jnp.zeros((bp, 8, SSD_CONV_DIM), F32)

    small = [[] for _ in range(4)]
    yp = ys = None
    ret_p = jnp.zeros((depth,) + zero_ret.shape[1:], F32)
    ssm_p = jnp.zeros((depth,) + zero_ssm.shape[1:], F32)
    ret_s = jnp.zeros(state_ret.shape, F32)
    ssm_s = jnp.zeros(state_ssm.shape, F32)
    f = ffn1_w_gu.shape[2] // 2
    w_in_b = w_in.astype(BF16)
    w_gd = _gate_dt_weight(w_in_b)
    for l in range(depth):
        abr, abi, bbr, bbi = _s5_discretize(s5_a_re[l], s5_a_im[l], s5_log_dt[l], s5_b_re[l], s5_b_im[l])
        w = dict(
            ffn1_norm=ffn1_norm[l], ffn1_wg=ffn1_w_gu[l, :, :f].astype(BF16), ffn1_wu=ffn1_w_gu[l, :, f:].astype(BF16),
            ffn1_wd=ffn1_w_down[l].astype(BF16), mix_norm=mix_norm[l], w_in=w_in_b, w_gd=w_gd,
            ret_ln_g=ret_ln_g[l], s5_wts=_s5_block_weights(bbr, bbi, s5_c_re[l], s5_c_im[l]), s5_abr=abr, s5_abi=abi,
            s5_d=s5_d[l], conv_w=ssd_conv_w[l], conv_b=ssd_conv_b[l].reshape(1, -1), dtb=_head_rows(ssd_dt_bias[l]),
            alog=_head_rows(ssd_a_log[l]), dskip=jnp.repeat(ssd_d[l], SSD_HEADDIM).reshape(1, SSD_INNER),
            ssd_norm=ssd_norm[l].reshape(1, SSD_INNER), ret_wo=ret_w_o[l].astype(BF16),
            s5_wglu=s5_w_glu[l].astype(BF16), ssd_wo=ssd_w_o[l].astype(BF16), w_out=w_out[l].astype(BF16),
            ffn2_norm=ffn2_norm[l], ffn2_wg=ffn2_w_gu[l, :, :f].astype(BF16), ffn2_wu=ffn2_w_gu[l, :, f:].astype(BF16),
            ffn2_wd=ffn2_w_down[l].astype(BF16), final_norm=final_norm,
        )
        xp, yp, ret_p, ssm_p, s1, c1 = _layer(xp, st_p, w, depth, l, zero_ret, zero_ssm, 0, ret_p, ssm_p,
                                              zero_s5, zero_s5, zero_conv)
        ss = state_s5[l].reshape(bs, S5_GROUPS * S5_STATE, 2)
        xs, ys, ret_s, ssm_s, s2, c2 = _layer(xs, st_s, w, depth, l, state_ret, state_ssm, l, ret_s, ssm_s,
                                              ss[..., 0], ss[..., 1], _pad_conv(state_conv[l]))
        for lst, v in zip(small, (s1, s2, c1, c2)):
            lst.append(v)

    s5_p, s5_s, conv_p, conv_s = (jnp.stack(o) for o in small)
    return (yp.reshape(bp, lp, d), ys.reshape(bs, ls, d), ret_p, ret_s, s5_p, s5_s, ssm_p, ssm_s, conv_p, conv_s)
```

```python
import functools

import numpy as np
import jax
import jax.numpy as jnp
from jax import lax
from jax.experimental import pallas as pl
from jax.experimental.pallas import tpu as pltpu

F32 = jnp.float32
BF16 = jnp.bfloat16

D_MODEL = 1024
PAST_LEN = 16384
EPS = 1e-6
RET_HEADS = 4
RET_DK = 128
RET_DV = 256
ROPE_BASE = 10000.0
S5_GROUPS = 64
S5_STATE = 64
SSD_INNER = 2048
SSD_HEADDIM = 64
SSD_HEADS = 32
SSD_GROUPS = 4
SSD_HPG = 8
SSD_STATE = 128
SSD_CONV = 4
SSD_CONV_DIM = 3072

ROWS = 128
VMEM_LIMIT = 56 * 1024 * 1024

P_Z, P_XBC, P_Q, P_K, P_V, P_G = 0, 2048, 5120, 5632, 6144, 7168
PG_GATE, PG_DT, PG_WIDTH = 0, 3072, 3584
W_DT, W_GATE = 9216, 9248
W_TILE = 1024
W_MAIN_TILES = (4, 5, 6, 7, 8, 0, 1, 2)
W_U_TILES = (3,)


def _cparams(sem):
    return pltpu.CompilerParams(dimension_semantics=sem, vmem_limit_bytes=VMEM_LIMIT)


def _pick(n, cands):
    for c in cands:
        if n % c == 0:
            return c
    raise ValueError(f"no tile for {n}")


def _resident(shape):
    nd = len(shape)
    return pl.BlockSpec(shape, lambda *_: (0,) * nd, pipeline_mode=pl.Buffered(1))


def _rms(x):
    return x * lax.rsqrt(jnp.mean(x * x, axis=-1, keepdims=True) + EPS)


def _silu(x):
    return x * jax.nn.sigmoid(x)


def _softplus(x):
    return jnp.maximum(x, 0.0) + jnp.log1p(jnp.exp(-jnp.abs(x)))


def _dot(a, b):
    return jnp.dot(a, b, preferred_element_type=F32)


def _dot_nt(a, b):
    return lax.dot_general(a, b, (((1,), (1,)), ((), ())), preferred_element_type=F32)


def _dot_tn(a, b):
    return lax.dot_general(a, b, (((0,), (0,)), ((), ())), preferred_element_type=F32)


def _dot_f32(a, b):
    return jnp.dot(a, b, preferred_element_type=F32, precision=lax.Precision.HIGHEST)


def _ffn_body(x_ref, ng_ref, wg_ref, wu_ref, wd_ref, *rest, n_chunks, tf):
    o_ref = rest[-2] if len(rest) == 3 else rest[0]
    x = x_ref[...]
    hb = (_rms(x) * ng_ref[...]).astype(BF16)
    acc = jnp.zeros(x.shape, F32)
    for c in range(n_chunks):
        sl = slice(c * tf, (c + 1) * tf)
        g = _dot(hb, wg_ref[:, sl])
        u = _dot(hb, wu_ref[:, sl])
        acc = acc + _dot((_silu(g) * u).astype(BF16), wd_ref[sl, :])
    y = x + 0.5 * acc
    o_ref[...] = y
    if len(rest) == 3:
        pg_ref, _, h_ref = rest
        h_ref[...] = (_rms(y) * pg_ref[...]).astype(h_ref.dtype)


def _ffn(x, norm_g, w_g, w_u, w_d, post_g=None, post_dtype=None):
    t, d = x.shape
    f = w_g.shape[1]
    tm = _pick(t, (512, 256, 128))
    n_chunks = 2
    tf = f // n_chunks
    row = pl.BlockSpec((1, d), lambda i: (0, 0))
    tile = pl.BlockSpec((tm, d), lambda i: (i, 0))
    with_post = post_g is not None
    res = pl.pallas_call(
        functools.partial(_ffn_body, n_chunks=n_chunks, tf=tf),
        grid=(t // tm,),
        in_specs=[tile, row, _resident((d, f)), _resident((d, f)), _resident((f, d))] + [row] * with_post,
        out_specs=[tile] + [tile] * with_post,
        out_shape=[jax.ShapeDtypeStruct((t, d), F32)] + [jax.ShapeDtypeStruct((t, d), post_dtype)] * with_post,
        compiler_params=_cparams(("parallel",)),
        name="ffn",
    )(x, norm_g.reshape(1, d), w_g, w_u, w_d, *([post_g.reshape(1, d)] if with_post else []))
    return res if with_post else (res[0], None)


def _mm_body(a_ref, w_ref, o_ref):
    o_ref[...] = _dot(a_ref[...], w_ref[...])


def _matmul(a, w, layer, tiles, tile_w=W_TILE, seq_len=None):
    t, k = a.shape
    tm = _pick(seq_len or t, (2048, 1024, 512, 256, 128))
    tn = tile_w
    n = len(tiles) * tn

    def src_tile(j):
        r = tiles[0]
        for dst, src in enumerate(tiles[1:], 1):
            r = jnp.where(j == dst, src, r)
        return r

    w_spec = pl.BlockSpec((None, k, tn), lambda i, j: (layer, 0, src_tile(j)))
    if seq_len is None:
        out_spec = pl.BlockSpec((tm, tn), lambda i, j: (i, j))
        out_shape = (t, n)
    else:
        assert tn == n
        nt = seq_len // tm
        out_spec = pl.BlockSpec((tm, n), lambda i, j: (i % nt, i // nt))
        out_shape = (seq_len, (t // seq_len) * n)
    return pl.pallas_call(
        _mm_body,
        grid=(t // tm, n // tn),
        in_specs=[pl.BlockSpec((tm, k), lambda i, j: (i, 0)), w_spec],
        out_specs=out_spec,
        out_shape=jax.ShapeDtypeStruct(out_shape, F32),
        compiler_params=_cparams(("parallel", "parallel")),
        name="in_proj",
    )(a, w)


def _merge_body(x_ref, oret_ref, s5_ref, ossd_ref, gl_ref, wro_ref, wglu_ref, wso_ref, wout_ref, o_ref):
    d = D_MODEL
    y_ret = _dot(oret_ref[...], wro_ref[...])
    yag = _dot(s5_ref[...], wglu_ref[...])
    y_s5 = yag[:, :d] * jax.nn.sigmoid(yag[:, d:])
    y_ssd = _dot(ossd_ref[...], wso_ref[...])
    gl = gl_ref[...]
    merged = (jax.nn.sigmoid(gl[:, :d]) * y_ret + jax.nn.sigmoid(gl[:, d:2 * d]) * y_s5
              + jax.nn.sigmoid(gl[:, 2 * d:]) * y_ssd)
    o_ref[...] = x_ref[...] + _dot(merged.astype(BF16), wout_ref[...])


def _merge(x, o_ret, s5_pre, o_ssd, p_main, w_ro, w_glu, w_so, w_out, s5_seq_len=None):
    t, d = x.shape
    tm = _pick(s5_seq_len or t, (512, 256, 128))
    if s5_seq_len is None:
        s5_spec = pl.BlockSpec((tm, d), lambda i: (i, 0))
    else:
        nt = s5_seq_len // tm
        s5_spec = pl.BlockSpec((tm, d), lambda i: (i % nt, i // nt))
    return pl.pallas_call(
        _merge_body,
        grid=(t // tm,),
        in_specs=[
            pl.BlockSpec((tm, d), lambda i: (i, 0)),
            pl.BlockSpec((tm, d), lambda i: (i, 0)),
            s5_spec,
            pl.BlockSpec((tm, 2 * d), lambda i: (i, 0)),
            pl.BlockSpec((tm, 3 * d), lambda i: (i, PG_GATE // (3 * d))),
            _resident((d, d)), _resident((d, 2 * d)), _resident((2 * d, d)), _resident((d, d)),
        ],
        out_specs=pl.BlockSpec((tm, d), lambda i: (i, 0)),
        out_shape=jax.ShapeDtypeStruct((t, d), F32),
        compiler_params=_cparams(("parallel",)),
        name="merge",
    )(x, o_ret, s5_pre, o_ssd, p_main, w_ro, w_glu, w_so, w_out)


def _seq_index(nseq):
    r = ROWS // nseq
    i = np.arange(ROWS)
    return i // r, i % r, r


def _block_masks(nseq):
    s, t, _ = _seq_index(nseq)
    same = s[:, None] == s[None, :]
    causal = same & (t[None, :] <= t[:, None])
    return jnp.asarray(causal, F32), jnp.asarray(same, F32)


def _retention_tables(nseq):
    s, t, r = _seq_index(nseq)
    lg = jnp.log1p(-jnp.exp2(-5.0 - jnp.arange(RET_HEADS, dtype=F32)))[:, None, None]
    tf = jnp.asarray(t, F32)
    diff = tf[:, None] - tf[None, :]
    causal = jnp.asarray((s[:, None] == s[None, :]) & (t[None, :] <= t[:, None]))
    decay = jnp.where(causal[None], jnp.exp(jnp.where(causal, diff, 0.0)[None] * lg), 0.0)
    qdec = jnp.broadcast_to(jnp.exp((tf + 1.0)[None, :, None] * lg), (RET_HEADS, ROWS, RET_DV))
    kdec = jnp.broadcast_to(jnp.exp((r - 1.0 - tf)[None, :, None] * lg), (RET_HEADS, ROWS, RET_DK))
    cdec = jnp.broadcast_to(jnp.exp(r * lg), (RET_HEADS, RET_DK, RET_DV))
    return decay, qdec, kdec, cdec


def _rotary_tables(pos):
    half = RET_DK // 2
    inv = ROPE_BASE ** (-jnp.arange(half, dtype=F32) / half)
    ang = pos.astype(F32)[:, None] * inv[None, :]
    cos, sin = jnp.cos(ang), jnp.sin(ang)
    return jnp.concatenate([cos, cos], axis=1), jnp.concatenate([-sin, sin], axis=1)


def _ret_body(q_ref, k_ref, v_ref, g_ref, cos_ref, sin_ref, dec_ref, qd_ref, kd_ref, cd_ref, lng_ref,
              s0_ref, *rest, nseq, hps):
    o_ref, s_ref, q_scr, kd_scr, cross_scr = rest[-5:]
    r = ROWS // nseq

    @pl.when(pl.program_id(2) == 0)
    def _():
        s_ref[...] = s0_ref[...]

    cos = cos_ref[...]
    sin = sin_ref[...]

    def rot(x):
        return x * cos + pltpu.roll(x, RET_DK // 2, 1) * sin

    for h in range(hps):
        ks = slice(h * RET_DK, (h + 1) * RET_DK)
        vs = slice(h * RET_DV, (h + 1) * RET_DV)
        q = rot(q_ref[:, ks])
        k = rot(k_ref[:, ks]) * (RET_DK ** -0.5)
        qb = q.astype(BF16)
        vb = v_ref[:, vs].astype(BF16)
        scores = _dot_nt(qb, k.astype(BF16)) * dec_ref[h]
        inner = _dot(scores.astype(BF16), vb)
        kd = k * kd_ref[h]
        cd = cd_ref[h]
        if nseq == 1:
            s = s_ref[0, h]
            cross = _dot(qb, s.astype(BF16))
            s_ref[0, h] = s * cd + _dot_tn(kd.astype(BF16), vb)
        else:
            q_scr[...] = q
            kd_scr[...] = kd

            def seq_step(i, carry, h=h, vs=vs, cd=cd):
                rs = pl.ds(pl.multiple_of(i * r, r), r)
                s = s_ref[i, h]
                cross_scr[rs, :] = _dot(q_scr[rs, :].astype(BF16), s.astype(BF16))
                s_ref[i, h] = s * cd + _dot_tn(kd_scr[rs, :].astype(BF16), v_ref[rs, vs].astype(BF16))
                return carry

            lax.fori_loop(0, nseq, seq_step, 0, unroll=4)
            cross = cross_scr[...]
        o = inner + cross * qd_ref[h]
        mu = jnp.mean(o, axis=-1, keepdims=True)
        oc = o - mu
        var = jnp.mean(oc * oc, axis=-1, keepdims=True)
        on = (oc * lax.rsqrt(var + EPS)) * lng_ref[:, vs]
        o_ref[:, vs] = (_silu(g_ref[:, vs]) * on).astype(o_ref.dtype)


def _stacked_state_io(state_all, lin, out_prev, lout, n_out, blk, idx):
    in_spec = pl.BlockSpec((None,) + blk, lambda *g: (lin,) + idx(*g))
    out_spec = pl.BlockSpec((None,) + blk, lambda *g: (lout,) + idx(*g))
    out_shape = jax.ShapeDtypeStruct((n_out,) + state_all.shape[1:], state_all.dtype)
    assert out_prev.shape == out_shape.shape
    return in_spec, out_spec, out_shape, [pl.BlockSpec(memory_space=pl.ANY)], [out_prev]


def _retention(p_main, nblk, nc, nseq, hps, cos, sin, tabs, ln_g, s0_all, lin, out_prev, lout, n_out):
    h = RET_HEADS
    hb = h // hps
    decay, qdec, kdec, cdec = tabs
    kw, vw = hps * RET_DK, hps * RET_DV
    s_in, s_out, s_shape, x_specs, x_args = _stacked_state_io(
        s0_all, lin, out_prev, lout, n_out, (nseq, hps, RET_DK, RET_DV), lambda b, hh, c: (b, hh, 0, 0))

    in_specs = [
        pl.BlockSpec((ROWS, kw), lambda b, hh, c: (b * nc + c, P_Q // kw + hh)),
        pl.BlockSpec((ROWS, kw), lambda b, hh, c: (b * nc + c, P_K // kw + hh)),
        pl.BlockSpec((ROWS, vw), lambda b, hh, c: (b * nc + c, P_V // vw + hh)),
        pl.BlockSpec((ROWS, vw), lambda b, hh, c: (b * nc + c, P_G // vw + hh)),
        pl.BlockSpec((ROWS, RET_DK), lambda b, hh, c: (c, 0)),
        pl.BlockSpec((ROWS, RET_DK), lambda b, hh, c: (c, 0)),
        pl.BlockSpec((hps, ROWS, ROWS), lambda b, hh, c: (hh, 0, 0)),
        pl.BlockSpec((hps, ROWS, RET_DV), lambda b, hh, c: (hh, 0, 0)),
        pl.BlockSpec((hps, ROWS, RET_DK), lambda b, hh, c: (hh, 0, 0)),
        pl.BlockSpec((hps, RET_DK, RET_DV), lambda b, hh, c: (hh, 0, 0)),
        pl.BlockSpec((1, vw), lambda b, hh, c: (0, hh)),
        s_in,
    ] + x_specs
    out_specs = [pl.BlockSpec((ROWS, vw), lambda b, hh, c: (b * nc + c, hh)), s_out]
    out_shape = [jax.ShapeDtypeStruct((nblk * nc * ROWS, h * RET_DV), BF16), s_shape]
    return pl.pallas_call(
        functools.partial(_ret_body, nseq=nseq, hps=hps),
        grid=(nblk, hb, nc),
        in_specs=in_specs, out_specs=out_specs, out_shape=out_shape,
        scratch_shapes=[pltpu.VMEM((ROWS, RET_DK), F32), pltpu.VMEM((ROWS, RET_DK), F32),
                        pltpu.VMEM((ROWS, RET_DV), F32)],
        input_output_aliases={len(in_specs) - 1: 1} if x_args else {},
        compiler_params=_cparams(("parallel", "parallel", "arbitrary")),
        name="retention",
    )(p_main, p_main, p_main, p_main, cos, sin, decay, qdec, kdec, cdec, ln_g.reshape(1, -1), s0_all, *x_args)


def _s5_disc_body(ar_ref, ai_ref, ldt_ref, br_ref, bi_ref, abr_ref, abi_ref, bbr_ref, bbi_ref):
    ar = ar_ref[...]
    ai = ai_ref[...]
    dt = jnp.exp(ldt_ref[...])
    mag = jnp.exp(dt * ar)
    abr = mag * jnp.cos(dt * ai)
    abi = mag * jnp.sin(dt * ai)
    nr = abr - 1.0
    ni = abi
    den = ar * ar + ai * ai
    f_re = (nr * ar + ni * ai) / den
    f_im = (ni * ar - nr * ai) / den
    br = br_ref[...]
    bi = bi_ref[...]
    abr_ref[...] = abr
    abi_ref[...] = abi
    bbr_ref[...] = f_re * br - f_im * bi
    bbi_ref[...] = f_re * bi + f_im * br


def _s5_discretize(a_re, a_im, log_dt, b_re, b_im):
    g, n, c = b_re.shape
    flat = (g * c * n // 128, 128)

    def rep(a):
        return jnp.broadcast_to(a[:, None, :], (g, c, n)).reshape(flat)

    args = (rep(a_re), rep(a_im), rep(jnp.broadcast_to(log_dt[:, None], (g, n))),
            jnp.swapaxes(b_re, 1, 2).reshape(flat), jnp.swapaxes(b_im, 1, 2).reshape(flat))
    spec = pl.BlockSpec(flat, lambda: (0, 0))
    abr, abi, bbr, bbi = pl.pallas_call(
        _s5_disc_body,
        in_specs=[spec] * 5, out_specs=[spec] * 4,
        out_shape=[jax.ShapeDtypeStruct(flat, F32)] * 4,
        name="s5_discretize",
    )(*args)
    abr = abr.reshape(g, c, n)[:, 0, :].reshape(1, g * n)
    abi = abi.reshape(g, c, n)[:, 0, :].reshape(1, g * n)
    return abr, abi, bbr.reshape(g, c, n), bbi.reshape(g, c, n)


def _s5_block_weights(bb_re, bb_im, c_re, c_im):
    g, c, n = bb_re.shape
    ng = 16
    eye = jnp.eye(ng, dtype=F32)

    def b_blk(bb):
        x = bb.reshape(g // ng, ng, c, n)
        return jnp.einsum("kgcn,gh->kgchn", x, eye).reshape(g // ng, ng * c, ng * n).astype(BF16)

    def c_blk(cc):
        x = cc.reshape(g // ng, ng, c, n)
        return jnp.einsum("kgcn,gh->kgnhc", x, eye).reshape(g // ng, ng * n, ng * c).astype(BF16)

    return b_blk(bb_re), b_blk(bb_im), c_blk(c_re), c_blk(c_im)


def _s5_body(u_ref, bre_ref, bim_ref, cre_ref, cim_ref, ar_ref, ai_ref, d_ref, s0r_ref, s0i_ref,
             y_ref, sr_ref, si_ref, bur, bui, *, tl, bb, wide):
    n_cg = bre_ref.shape[0]
    cw = bre_ref.shape[1]
    sw = bre_ref.shape[2]
    w = n_cg * cw
    lc = 512

    @pl.when(pl.program_id(1) == 0)
    def _():
        sr_ref[...] = s0r_ref[...]
        si_ref[...] = s0i_ref[...]

    if wide:
        ux = u_ref[...]
        u = jnp.swapaxes(jnp.stack([ux[:, b * w:(b + 1) * w] for b in range(bb)], axis=0), 0, 1)
        u = u.reshape(tl * bb, w)
    else:
        u = u_ref[...]
    ub = u.astype(BF16)
    for cg in range(n_cg):
        ucg = ub[:, cg * cw:(cg + 1) * cw]
        bur[:, cg * sw:(cg + 1) * sw] = _dot(ucg, bre_ref[cg])
        bui[:, cg * sw:(cg + 1) * sw] = _dot(ucg, bim_ref[cg])

    for bt in range(bb // 8):
        rs = slice(bt * 8, (bt + 1) * 8)
        for ci in range(n_cg * sw // lc):
            cs = slice(ci * lc, (ci + 1) * lc)
            ar = jnp.broadcast_to(ar_ref[:, cs], (8, lc))
            ai = jnp.broadcast_to(ai_ref[:, cs], (8, lc))

            def step(l, carry, cs=cs, ar=ar, ai=ai, bt=bt):
                xr, xi = carry
                r0 = pl.multiple_of(l * bb + bt * 8, 8)
                nr = (ar * xr - ai * xi) + bur[pl.ds(r0, 8), cs]
                ni = (ar * xi + ai * xr) + bui[pl.ds(r0, 8), cs]
                bur[pl.ds(r0, 8), cs] = nr
                bui[pl.ds(r0, 8), cs] = ni
                return nr, ni

            xr, xi = lax.fori_loop(0, tl, step, (sr_ref[rs, cs], si_ref[rs, cs]), unroll=2)
            sr_ref[rs, cs] = xr
            si_ref[rs, cs] = xi

    for cg in range(n_cg):
        xs = slice(cg * sw, (cg + 1) * sw)
        us = slice(cg * cw, (cg + 1) * cw)
        y = _dot(bur[:, xs].astype(BF16), cre_ref[cg]) - _dot(bui[:, xs].astype(BF16), cim_ref[cg])
        y = jax.nn.gelu(y + d_ref[:, us] * u[:, us])
        if wide:
            y = jnp.swapaxes(y.reshape(tl, bb, cw), 0, 1)
            for b in range(bb):
                y_ref[:, b * w + cg * cw:b * w + (cg + 1) * cw] = y[b].astype(y_ref.dtype)
        else:
            y_ref[:, us] = y.astype(y_ref.dtype)


def _s5(u_tm, wts, abr, abi, d, s0r, s0i, tl, bb):
    b, ns = s0r.shape
    bre, bim, cre, cim = wts
    w = bre.shape[0] * bre.shape[1]
    wide = u_tm.shape[1] != w
    rows = tl * bb
    nbb = b // bb
    if wide:
        assert bb == b and u_tm.shape[1] == b * w
        ntb = u_tm.shape[0] // tl
        io_spec = pl.BlockSpec((tl, b * w), lambda i, j: (j, 0))
    else:
        ntb = u_tm.shape[0] // (rows * nbb)
        io_spec = pl.BlockSpec((rows, w), lambda i, j: (i * ntb + j, 0))
    return pl.pallas_call(
        functools.partial(_s5_body, tl=tl, bb=bb, wide=wide),
        grid=(nbb, ntb),
        in_specs=[
            io_spec,
            _resident(bre.shape), _resident(bim.shape), _resident(cre.shape), _resident(cim.shape),
            _resident((1, ns)), _resident((1, ns)), _resident((1, w)),
            pl.BlockSpec((bb, ns), lambda i, j: (i, 0)),
            pl.BlockSpec((bb, ns), lambda i, j: (i, 0)),
        ],
        out_specs=[
            io_spec,
            pl.BlockSpec((bb, ns), lambda i, j: (i, 0)),
            pl.BlockSpec((bb, ns), lambda i, j: (i, 0)),
        ],
        out_shape=[
            jax.ShapeDtypeStruct(u_tm.shape, BF16),
            jax.ShapeDtypeStruct((b, ns), F32),
            jax.ShapeDtypeStruct((b, ns), F32),
        ],
        scratch_shapes=[pltpu.VMEM((rows, ns), F32), pltpu.VMEM((rows, ns), F32)],
        compiler_params=_cparams(("parallel", "arbitrary")),
        name="s5",
    )(u_tm, bre, bim, cre, cim, abr, abi, d.reshape(1, w), s0r, s0i)


def _to_time_major(u, b, l, bb):
    w = u.shape[-1]
    return u.reshape(b // bb, bb, l, w).transpose(0, 2, 1, 3).reshape(b * l, w)


def _from_time_major(y, b, l, bb):
    w = y.shape[-1]
    return y.reshape(b // bb, l, bb, w).transpose(0, 2, 1, 3).reshape(b * l, w)


def _ssd_body(z_ref, xs_ref, bm_ref, cm_ref, dt_ref, px_ref, pb_ref, pc_ref, wx_ref, wb_ref, wc_ref,
              bx_ref, bb_ref, bc_ref, dtb_ref, alog_ref, dsk_ref, ng_ref, tri_ref, trit_ref, seqm_ref, h0_ref,
              *rest, nseq, gps):
    o_ref, h_ref, tx, tb, tc, xsw_scr, bm_scr, cm_scr, col_scr, yoff_scr = rest[-10:]

    @pl.when(pl.program_id(2) == 0)
    def _():
        h_ref[...] = h0_ref[...]
        if nseq == 1:
            tx[...] = px_ref[0]
            tb[...] = pb_ref[0]
            tc[...] = pc_ref[0]

    for gi in range(gps):
        _ssd_group(gi, z_ref, xs_ref, bm_ref, cm_ref, dt_ref, px_ref, pb_ref, pc_ref, wx_ref, wb_ref, wc_ref,
                   bx_ref, bb_ref, bc_ref, dtb_ref, alog_ref, dsk_ref, ng_ref, tri_ref, trit_ref, seqm_ref,
                   o_ref, h_ref, tx, tb, tc, xsw_scr, bm_scr, cm_scr, col_scr, yoff_scr, nseq)


def _ssd_group(gi, z_ref, xs_ref, bm_ref, cm_ref, dt_ref, px_ref, pb_ref, pc_ref, wx_ref, wb_ref, wc_ref,
               bx_ref, bb_ref, bc_ref, dtb_ref, alog_ref, dsk_ref, ng_ref, tri_ref, trit_ref, seqm_ref,
               o_ref, h_ref, tx, tb, tc, xsw_scr, bm_scr, cm_scr, col_scr, yoff_scr, nseq):
    r = ROWS // nseq
    pd = SSD_HEADDIM
    nh = SSD_HPG
    gw = nh * pd
    gc = slice(gi * gw, (gi + 1) * gw)
    gn = slice(gi * SSD_STATE, (gi + 1) * SSD_STATE)
    gh = slice(gi * nh, (gi + 1) * nh)
    row = lax.broadcasted_iota(jnp.int32, (ROWS, 1), 0)
    t_idx = row & (r - 1)
    lo = lax.broadcasted_iota(jnp.int32, (1, ROWS), 1) < pd

    def conv(x_ref, tail, prev_ref, w_ref, b_ref, cs):
        x = x_ref[:, cs]
        acc = b_ref[:, cs] + w_ref[SSD_CONV - 1:SSD_CONV, cs] * x
        if nseq == 1:
            xe = jnp.concatenate([tail[:, cs], x], axis=0)
            tail[:, cs] = x[ROWS - 8:, :]
        else:
            z8 = prev_ref[:, :, cs].reshape(ROWS, x.shape[1])
        for k in range(1, SSD_CONV):
            if nseq == 1:
                xk = xe[8 - k:8 - k + ROWS, :]
            else:
                xk = jnp.where(t_idx >= k, pltpu.roll(x, k, 0), pltpu.roll(z8, ROWS - 8 + k, 0))
            acc = acc + w_ref[SSD_CONV - 1 - k:SSD_CONV - k, cs] * xk
        return _silu(acc)

    xs = conv(xs_ref, tx, px_ref, wx_ref, bx_ref, gc)
    bm = conv(bm_ref, tb, pb_ref, wb_ref, bb_ref, gn)
    cm = conv(cm_ref, tc, pc_ref, wc_ref, bc_ref, gn)
    bmb = bm.astype(BF16)
    cmb = cm.astype(BF16)
    xsb = xs.astype(BF16)

    dt8 = _softplus(dt_ref[:, gn].T[0:nh, :] + dtb_ref[gh, :])
    dta8 = dt8 * (-jnp.exp(alog_ref[gh, :]))
    cum8 = _dot_f32(dta8, trit_ref[...])
    clast8 = _dot_f32(dta8, seqm_ref[...])
    e8 = jnp.exp(cum8)
    dtw8 = dt8 * jnp.exp(clast8 - cum8)
    ecl8 = jnp.exp(clast8)
    colf = jnp.concatenate([cum8, e8, dtw8, ecl8, jnp.zeros((ROWS - 4 * nh, ROWS), F32)], axis=0).T
    c_cum, c_e, c_dtw, c_ecl = 0, nh, 2 * nh, 3 * nh
    causal = tri_ref[...] > 0.0
    cb = _dot_nt(cmb, bmb)

    def expand(off):
        return jnp.concatenate(
            [jnp.where(lo, colf[:, off + 2 * p:off + 2 * p + 1], colf[:, off + 2 * p + 1:off + 2 * p + 2])
             for p in range(nh // 2)], axis=1)

    ys = []
    zero = jnp.zeros((), BF16)
    for p in range(nh // 2):
        ms = []
        for h in (2 * p, 2 * p + 1):
            seg = colf[:, c_cum + h:c_cum + h + 1] - cum8[h:h + 1, :]
            lm = jnp.where(causal, jnp.exp(seg), 0.0)
            ms.append(((cb * lm) * dt8[h:h + 1, :]).astype(BF16))
        xp = xsb[:, p * 2 * pd:(p + 1) * 2 * pd]
        xcat = jnp.concatenate([jnp.where(lo, xp, zero), jnp.where(lo, zero, xp)], axis=0)
        ys.append(_dot(jnp.concatenate(ms, axis=1), xcat))
    y = jnp.concatenate(ys, axis=1)

    xsw = xs * expand(c_dtw)

    def new_state(hg, col_row, upd):
        parts = [hg[h * pd:(h + 1) * pd, :] * col_row[:, c_ecl + h:c_ecl + h + 1] for h in range(nh)]
        return (jnp.concatenate(parts, axis=0) + upd).reshape(nh, pd, SSD_STATE)

    if nseq == 1:
        hg = h_ref[0, gh].reshape(gw, SSD_STATE)
        yoff = _dot_nt(cmb, hg.astype(BF16))
        h_ref[0, gh] = new_state(hg, colf[0:1, :], _dot_tn(xsw.astype(BF16), bmb))
    else:
        xsw_scr[...] = xsw
        bm_scr[...] = bm
        cm_scr[...] = cm
        col_scr[...] = colf

        def seq_step(i, carry):
            r0 = pl.multiple_of(i * r, r)
            rs = pl.ds(r0, r)
            hg = h_ref[i, gh].reshape(gw, SSD_STATE)
            yoff_scr[rs, :] = _dot_nt(cm_scr[rs, :].astype(BF16), hg.astype(BF16))
            upd = _dot_tn(xsw_scr[rs, :].astype(BF16), bm_scr[rs, :].astype(BF16))
            h_ref[i, gh] = new_state(hg, col_scr[pl.ds(r0, 1), :], upd)
            return carry

        lax.fori_loop(0, nseq, seq_step, 0, unroll=8)
        yoff = yoff_scr[...]

    y = y + yoff * expand(c_e) + dsk_ref[:, gc] * xs
    y = y * _silu(z_ref[:, gc])
    o_ref[:, gc] = (_rms(y) * ng_ref[:, gc]).astype(o_ref.dtype)


def _ssd(p_main, p_dt, nblk, nc, nseq, gps, conv0, conv_w, conv_b, dtb, alog, dskip, norm_g,
         h0_all, lin, out_prev, lout, n_out):
    g = SSD_GROUPS // gps
    gw = gps * SSD_INNER // SSD_GROUPS
    ns = gps * SSD_STATE
    nh = gps * SSD_HPG
    tri, seqm = _block_masks(nseq)
    h_in, h_out, h_shape, x_specs, x_args = _stacked_state_io(
        h0_all, lin, out_prev, lout, n_out, (nseq, nh, SSD_HEADDIM, SSD_STATE), lambda b, gg, c: (b, gg, 0, 0))
    xo = P_XBC
    bo = P_XBC + SSD_INNER
    co = bo + SSD_GROUPS * SSD_STATE

    def cols(width, off):
        return lambda b, gg, c: (0, off // width + gg)

    in_specs = [
        pl.BlockSpec((ROWS, gw), lambda b, gg, c: (b * nc + c, P_Z // gw + gg)),
        pl.BlockSpec((ROWS, gw), lambda b, gg, c: (b * nc + c, xo // gw + gg)),
        pl.BlockSpec((ROWS, ns), lambda b, gg, c: (b * nc + c, bo // ns + gg)),
        pl.BlockSpec((ROWS, ns), lambda b, gg, c: (b * nc + c, co // ns + gg)),
        pl.BlockSpec((ROWS, ns), lambda b, gg, c: (b * nc + c, PG_DT // ns + gg)),
        pl.BlockSpec((nseq, 8, gw), lambda b, gg, c: (b, 0, gg)),
        pl.BlockSpec((nseq, 8, ns), lambda b, gg, c: (b, 0, (bo - xo) // ns + gg)),
        pl.BlockSpec((nseq, 8, ns), lambda b, gg, c: (b, 0, (co - xo) // ns + gg)),
        pl.BlockSpec((SSD_CONV, gw), cols(gw, 0)),
        pl.BlockSpec((SSD_CONV, ns), cols(ns, bo - xo)),
        pl.BlockSpec((SSD_CONV, ns), cols(ns, co - xo)),
        pl.BlockSpec((1, gw), cols(gw, 0)),
        pl.BlockSpec((1, ns), cols(ns, bo - xo)),
        pl.BlockSpec((1, ns), cols(ns, co - xo)),
        pl.BlockSpec((nh, ROWS), lambda b, gg, c: (gg, 0)),
        pl.BlockSpec((nh, ROWS), lambda b, gg, c: (gg, 0)),
        pl.BlockSpec((1, gw), cols(gw, 0)),
        pl.BlockSpec((1, gw), cols(gw, 0)),
        pl.BlockSpec((ROWS, ROWS), lambda b, gg, c: (0, 0)),
        pl.BlockSpec((ROWS, ROWS), lambda b, gg, c: (0, 0)),
        pl.BlockSpec((ROWS, ROWS), lambda b, gg, c: (0, 0)),
        h_in,
    ] + x_specs
    out_specs = [pl.BlockSpec((ROWS, gw), lambda b, gg, c: (b * nc + c, gg)), h_out]
    out_shape = [jax.ShapeDtypeStruct((nblk * nc * ROWS, SSD_INNER), BF16), h_shape]
    assert nseq == 1 or gps == 1
    gw1 = SSD_INNER // SSD_GROUPS
    scratch = [pltpu.VMEM((8, gw), F32), pltpu.VMEM((8, ns), F32), pltpu.VMEM((8, ns), F32),
               pltpu.VMEM((ROWS, gw1), F32), pltpu.VMEM((ROWS, SSD_STATE), F32), pltpu.VMEM((ROWS, SSD_STATE), F32),
               pltpu.VMEM((ROWS, 128), F32), pltpu.VMEM((ROWS, gw1), F32)]
    return pl.pallas_call(
        functools.partial(_ssd_body, nseq=nseq, gps=gps),
        grid=(nblk, g, nc),
        in_specs=in_specs, out_specs=out_specs, out_shape=out_shape,
        scratch_shapes=scratch,
        input_output_aliases={len(in_specs) - 1: 1} if x_args else {},
        compiler_params=_cparams(("parallel", "parallel", "arbitrary")),
        name="ssd",
    )(p_main, p_main, p_main, p_main, p_dt, conv0, conv0, conv0, conv_w, conv_w, conv_w,
      conv_b, conv_b, conv_b, dtb, alog, dskip, norm_g, tri, tri.T, seqm, h0_all, *x_args)


def _head_rows(v):
    return jnp.broadcast_to(v[:, None], (SSD_HEADS, ROWS))


def _gate_dt_weight(w_in_b):
    nl, k, _ = w_in_b.shape
    w_dt = w_in_b[:, :, W_DT:W_GATE].reshape(nl, k, SSD_GROUPS, SSD_HPG)
    w_dt = jnp.pad(w_dt, ((0, 0), (0, 0), (0, 0), (0, 128 - SSD_HPG))).reshape(nl, k, SSD_GROUPS * 128)
    return jnp.concatenate([w_in_b[:, :, W_GATE:], w_dt], axis=2)


def _pad_conv(c):
    return jnp.pad(c, ((0, 0), (8 - (SSD_CONV - 1), 0), (0, 0)))


class _Stream:
    def __init__(self, b, l, pos0, s5_tl, s5_bb, ret_hps, ssd_gps):
        self.b, self.l = b, l
        self.nseq = 1 if l % ROWS == 0 else ROWS // l
        self.nc = max(l // ROWS, 1)
        self.nblk = b // self.nseq
        self.s5_tl, self.s5_bb, self.ret_hps, self.ssd_gps = s5_tl, s5_bb, ret_hps, ssd_gps
        pos = pos0 + jnp.arange(self.nc * ROWS, dtype=jnp.int32) % l
        self.cos, self.sin = _rotary_tables(pos)
        self.rtab = _retention_tables(self.nseq)


def _layer(x, st, w, depth, lout, ret_all, ssm_all, lin, ret_prev, ssm_prev, s5r0, s5i0, conv0):
    b, l = st.b, st.l
    whole_seq = st.nseq == 1
    x, h_mix = _ffn(x, w["ffn1_norm"], w["ffn1_wg"], w["ffn1_wu"], w["ffn1_wd"], w["mix_norm"], BF16)
    p_main = _matmul(h_mix, w["w_in"], layer=lout, tiles=W_MAIN_TILES)
    p_gd = _matmul(h_mix, w["w_gd"], layer=lout, tiles=(0, 1), tile_w=PG_WIDTH // 2)

    o_ret, ret_out = _retention(p_main, st.nblk, st.nc, st.nseq, st.ret_hps, st.cos, st.sin, st.rtab,
                                w["ret_ln_g"], ret_all, lin, ret_prev, lout, depth)

    if whole_seq:
        u_tm = _matmul(h_mix, w["w_in"], seq_len=l, layer=lout, tiles=W_U_TILES)
    else:
        u_tm = _to_time_major(_matmul(h_mix, w["w_in"], layer=lout, tiles=W_U_TILES), b, l, st.s5_bb)
    y5, s5r, s5i = _s5(u_tm, w["s5_wts"], w["s5_abr"], w["s5_abi"], w["s5_d"], s5r0, s5i0, st.s5_tl, st.s5_bb)
    s5_pre = y5 if whole_seq else _from_time_major(y5, b, l, st.s5_bb)

    o_ssd, ssm_out = _ssd(p_main, p_gd, st.nblk, st.nc, st.nseq, st.ssd_gps, conv0, w["conv_w"], w["conv_b"],
                          w["dtb"], w["alog"], w["dskip"], w["ssd_norm"], ssm_all, lin, ssm_prev, lout, depth)
    conv_new = p_main.reshape(b, l, -1)[:, l - (SSD_CONV - 1):, P_XBC:P_XBC + SSD_CONV_DIM]

    x = _merge(x, o_ret, s5_pre, o_ssd, p_gd, w["ret_wo"], w["s5_wglu"], w["ssd_wo"], w["w_out"],
               s5_seq_len=l if whole_seq else None)
    x, y_fin = _ffn(x, w["ffn2_norm"], w["ffn2_wg"], w["ffn2_wu"], w["ffn2_wd"],
                    w["final_norm"] if lout == depth - 1 else None, F32)
    s5_new = jnp.stack([s5r, s5i], axis=-1).reshape(b, S5_GROUPS, S5_STATE, 2)
    return x, y_fin, ret_out, ssm_out, s5_new, conv_new


def kernel(x_prompt, x_sample, state_ret, state_s5, state_ssm, state_conv, ffn1_norm, ffn1_w_gu, ffn1_w_down, mix_norm, w_in, ret_ln_g, ret_w_o, s5_a_re, s5_a_im, s5_log_dt, s5_b_re, s5_b_im, s5_c_re, s5_c_im, s5_d, s5_w_glu, ssd_conv_w, ssd_conv_b, ssd_dt_bias, ssd_a_log, ssd_d, ssd_norm, ssd_w_o, w_out, ffn2_norm, ffn2_w_gu, ffn2_w_down, final_norm):
    depth = w_in.shape[0]
    bp, lp, d = x_prompt.shape
    bs, ls, _ = x_sample.shape
    assert lp % ROWS == 0 and ROWS % ls == 0 and bs % (ROWS // ls) == 0 and bp % 8 == 0 and bs % 32 == 0
    st_p = _Stream(bp, lp, 0, s5_tl=64, s5_bb=bp, ret_hps=RET_HEADS, ssd_gps=SSD_GROUPS)
    st_s = _Stream(bs, ls, PAST_LEN, s5_tl=ls, s5_bb=32, ret_hps=2, ssd_gps=1)

    xp = x_prompt.reshape(bp * lp, d)
    xs = x_sample.reshape(bs * ls, d)
    zero_ret = jnp.zeros((1, bp, RET_HEADS, RET_DK, RET_DV), F32)
    zero_s5 = jnp.zeros((bp, S5_GROUPS * S5_STATE), F32)
    zero_ssm = jnp.zeros((1, bp, SSD_HEADS, SSD_HEADDIM, SSD_STATE), F32)
    zero_conv = jnp.zeros((bp, 8, SSD_CONV_DIM), F32)

    small = [[] for _ in range(4)]
    yp = ys = None
    ret_p = jnp.zeros((depth,) + zero_ret.shape[1:], F32)
    ssm_p = jnp.zeros((depth,) + zero_ssm.shape[1:], F32)
    ret_s = jnp.zeros(state_ret.shape, F32)
    ssm_s = jnp.zeros(state_ssm.shape, F32)
    f = ffn1_w_gu.shape[2] // 2
    w_in_b = w_in.astype(BF16)
    w_gd = _gate_dt_weight(w_in_b)
    for l in range(depth):
        abr, abi, bbr, bbi = _s5_discretize(s5_a_re[l], s5_a_im[l], s5_log_dt[l], s5_b_re[l], s5_b_im[l])
        w = dict(
            ffn1_norm=ffn1_norm[l], ffn1_wg=ffn1_w_gu[l, :, :f].astype(BF16), ffn1_wu=ffn1_w_gu[l, :, f:].astype(BF16),
            ffn1_wd=ffn1_w_down[l].astype(BF16), mix_norm=mix_norm[l], w_in=w_in_b, w_gd=w_gd,
            ret_ln_g=ret_ln_g[l], s5_wts=_s5_block_weights(bbr, bbi, s5_c_re[l], s5_c_im[l]), s5_abr=abr, s5_abi=abi,
            s5_d=s5_d[l], conv_w=ssd_conv_w[l], conv_b=ssd_conv_b[l].reshape(1, -1), dtb=_head_rows(ssd_dt_bias[l]),
            alog=_head_rows(ssd_a_log[l]), dskip=jnp.repeat(ssd_d[l], SSD_HEADDIM).reshape(1, SSD_INNER),
            ssd_norm=ssd_norm[l].reshape(1, SSD_INNER), ret_wo=ret_w_o[l].astype(BF16),
            s5_wglu=s5_w_glu[l].astype(BF16), ssd_wo=ssd_w_o[l].astype(BF16), w_out=w_out[l].astype(BF16),
            ffn2_norm=ffn2_norm[l], ffn2_wg=ffn2_w_gu[l, :, :f].astype(BF16), ffn2_wu=ffn2_w_gu[l, :, f:].astype(BF16),
            ffn2_wd=ffn2_w_down[l].astype(BF16), final_norm=final_norm,
        )
        xp, yp, ret_p, ssm_p, s1, c1 = _layer(xp, st_p, w, depth, l, zero_ret, zero_ssm, 0, ret_p, ssm_p,
                                              zero_s5, zero_s5, zero_conv)
        ss = state_s5[l].reshape(bs, S5_GROUPS * S5_STATE, 2)
        xs, ys, ret_s, ssm_s, s2, c2 = _layer(xs, st_s, w, depth, l, state_ret, state_ssm, l, ret_s, ssm_s,
                                              ss[..., 0], ss[..., 1], _pad_conv(state_conv[l]))
        for lst, v in zip(small, (s1, s2, c1, c2)):
            lst.append(v)

    s5_p, s5_s, conv_p, conv_s = (jnp.stack(o) for o in small)
    return (yp.reshape(bp, lp, d), ys.reshape(bs, ls, d), ret_p, ret_s, s5_p, s5_s, ssm_p, ssm_s, conv_p, conv_s)
```

```python
import functools

import numpy as np
import jax
import jax.numpy as jnp
from jax import lax
from jax.experimental import pallas as pl
from jax.experimental.pallas import tpu as pltpu

F32 = jnp.float32
BF16 = jnp.bfloat16

D_MODEL = 1024
PAST_LEN = 16384
EPS = 1e-6
RET_HEADS = 4
RET_DK = 128
RET_DV = 256
ROPE_BASE = 10000.0
S5_GROUPS = 64
S5_STATE = 64
SSD_INNER = 2048
SSD_HEADDIM = 64
SSD_HEADS = 32
SSD_GROUPS = 4
SSD_HPG = 8
SSD_STATE = 128
SSD_CONV = 4
SSD_CONV_DIM = 3072

ROWS = 128
VMEM_LIMIT = 56 * 1024 * 1024

P_Z, P_XBC, P_Q, P_K, P_V, P_G = 0, 2048, 5120, 5632, 6144, 7168
PG_GATE, PG_DT, PG_WIDTH = 0, 3072, 3584
W_DT, W_GATE = 9216, 9248
W_TILE = 1024
W_MAIN_TILES = (4, 5, 6, 7, 8, 0, 1, 2)
W_U_TILES = (3,)


def _cparams(sem):
    return pltpu.CompilerParams(dimension_semantics=sem, vmem_limit_bytes=VMEM_LIMIT)


def _pick(n, cands):
    for c in cands:
        if n % c == 0:
            return c
    raise ValueError(f"no tile for {n}")


def _resident(shape):
    nd = len(shape)
    return pl.BlockSpec(shape, lambda *_: (0,) * nd, pipeline_mode=pl.Buffered(1))


def _rms(x):
    return x * lax.rsqrt(jnp.mean(x * x, axis=-1, keepdims=True) + EPS)


def _silu(x):
    h = 0.5 * x
    return h + h * jnp.tanh(h)


def _softplus(x):
    return jnp.maximum(x, 0.0) + jnp.log1p(jnp.exp(-jnp.abs(x)))


def _dot(a, b):
    return jnp.dot(a, b, preferred_element_type=F32)


def _dot_nt(a, b):
    return lax.dot_general(a, b, (((1,), (1,)), ((), ())), preferred_element_type=F32)


def _dot_tn(a, b):
    return lax.dot_general(a, b, (((0,), (0,)), ((), ())), preferred_element_type=F32)


def _dot_f32(a, b):
    return jnp.dot(a, b, preferred_element_type=F32, precision=lax.Precision.HIGHEST)


def _ffn_body(x_ref, ng_ref, wg_ref, wu_ref, wd_ref, *rest, n_chunks, tf):
    o_ref = rest[-2] if len(rest) == 3 else rest[0]
    x = x_ref[...]
    hb = (_rms(x) * ng_ref[...]).astype(BF16)
    acc = jnp.zeros(x.shape, F32)
    for c in range(n_chunks):
        sl = slice(c * tf, (c + 1) * tf)
        g = _dot(hb, wg_ref[:, sl])
        u = _dot(hb, wu_ref[:, sl])
        acc = acc + _dot((_silu(g) * u).astype(BF16), wd_ref[sl, :])
    y = x + 0.5 * acc
    o_ref[...] = y
    if len(rest) == 3:
        pg_ref, _, h_ref = rest
        h_ref[...] = (_rms(y) * pg_ref[...]).astype(h_ref.dtype)


def _ffn(x, norm_g, w_g, w_u, w_d, post_g=None, post_dtype=None):
    t, d = x.shape
    f = w_g.shape[1]
    tm = _pick(t, (512, 256, 128))
    n_chunks = 2
    tf = f // n_chunks
    row = pl.BlockSpec((1, d), lambda i: (0, 0))
    tile = pl.BlockSpec((tm, d), lambda i: (i, 0))
    with_post = post_g is not None
    res = pl.pallas_call(
        functools.partial(_ffn_body, n_chunks=n_chunks, tf=tf),
        grid=(t // tm,),
        in_specs=[tile, row, _resident((d, f)), _resident((d, f)), _resident((f, d))] + [row] * with_post,
        out_specs=[tile] + [tile] * with_post,
        out_shape=[jax.ShapeDtypeStruct((t, d), F32)] + [jax.ShapeDtypeStruct((t, d), post_dtype)] * with_post,
        compiler_params=_cparams(("parallel",)),
        name="ffn",
    )(x, norm_g.reshape(1, d), w_g, w_u, w_d, *([post_g.reshape(1, d)] if with_post else []))
    return res if with_post else (res[0], None)


def _mm_body(a_ref, w_ref, o_ref):
    o_ref[...] = _dot(a_ref[...], w_ref[...])


def _matmul(a, w, layer, tiles, tile_w=W_TILE, seq_len=None):
    t, k = a.shape
    tm = _pick(seq_len or t, (2048, 1024, 512, 256, 128))
    tn = tile_w
    n = len(tiles) * tn

    def src_tile(j):
        r = tiles[0]
        for dst, src in enumerate(tiles[1:], 1):
            r = jnp.where(j == dst, src, r)
        return r

    w_spec = pl.BlockSpec((None, k, tn), lambda i, j: (layer, 0, src_tile(j)))
    if seq_len is None:
        out_spec = pl.BlockSpec((tm, tn), lambda i, j: (i, j))
        out_shape = (t, n)
    else:
        assert tn == n
        nt = seq_len // tm
        out_spec = pl.BlockSpec((tm, n), lambda i, j: (i % nt, i // nt))
        out_shape = (seq_len, (t // seq_len) * n)
    return pl.pallas_call(
        _mm_body,
        grid=(t // tm, n // tn),
        in_specs=[pl.BlockSpec((tm, k), lambda i, j: (i, 0)), w_spec],
        out_specs=out_spec,
        out_shape=jax.ShapeDtypeStruct(out_shape, F32),
        compiler_params=_cparams(("parallel", "parallel")),
        name="in_proj",
    )(a, w)


def _merge_body(x_ref, oret_ref, s5_ref, ossd_ref, gl_ref, wro_ref, wglu_ref, wso_ref, wout_ref, o_ref):
    d = D_MODEL
    y_ret = _dot(oret_ref[...], wro_ref[...])
    yag = _dot(s5_ref[...], wglu_ref[...])
    y_s5 = yag[:, :d] * jax.nn.sigmoid(yag[:, d:])
    y_ssd = _dot(ossd_ref[...], wso_ref[...])
    gl = gl_ref[...]
    merged = (jax.nn.sigmoid(gl[:, :d]) * y_ret + jax.nn.sigmoid(gl[:, d:2 * d]) * y_s5
              + jax.nn.sigmoid(gl[:, 2 * d:]) * y_ssd)
    o_ref[...] = x_ref[...] + _dot(merged.astype(BF16), wout_ref[...])


def _merge(x, o_ret, s5_pre, o_ssd, p_main, w_ro, w_glu, w_so, w_out, s5_seq_len=None):
    t, d = x.shape
    tm = _pick(s5_seq_len or t, (512, 256, 128))
    if s5_seq_len is None:
        s5_spec = pl.BlockSpec((tm, d), lambda i: (i, 0))
    else:
        nt = s5_seq_len // tm
        s5_spec = pl.BlockSpec((tm, d), lambda i: (i % nt, i // nt))
    return pl.pallas_call(
        _merge_body,
        grid=(t // tm,),
        in_specs=[
            pl.BlockSpec((tm, d), lambda i: (i, 0)),
            pl.BlockSpec((tm, d), lambda i: (i, 0)),
            s5_spec,
            pl.BlockSpec((tm, 2 * d), lambda i: (i, 0)),
            pl.BlockSpec((tm, 3 * d), lambda i: (i, PG_GATE // (3 * d))),
            _resident((d, d)), _resident((d, 2 * d)), _resident((2 * d, d)), _resident((d, d)),
        ],
        out_specs=pl.BlockSpec((tm, d), lambda i: (i, 0)),
        out_shape=jax.ShapeDtypeStruct((t, d), F32),
        compiler_params=_cparams(("parallel",)),
        name="merge",
    )(x, o_ret, s5_pre, o_ssd, p_main, w_ro, w_glu, w_so, w_out)


def _seq_index(nseq):
    r = ROWS // nseq
    i = np.arange(ROWS)
    return i // r, i % r, r


def _block_masks(nseq):
    s, t, _ = _seq_index(nseq)
    same = s[:, None] == s[None, :]
    causal = same & (t[None, :] <= t[:, None])
    return jnp.asarray(causal, F32), jnp.asarray(same, F32)


def _retention_tables(nseq):
    s, t, r = _seq_index(nseq)
    lg = jnp.log1p(-jnp.exp2(-5.0 - jnp.arange(RET_HEADS, dtype=F32)))[:, None, None]
    tf = jnp.asarray(t, F32)
    diff = tf[:, None] - tf[None, :]
    causal = jnp.asarray((s[:, None] == s[None, :]) & (t[None, :] <= t[:, None]))
    decay = jnp.where(causal[None], jnp.exp(jnp.where(causal, diff, 0.0)[None] * lg), 0.0)
    qdec = jnp.broadcast_to(jnp.exp((tf + 1.0)[None, :, None] * lg), (RET_HEADS, ROWS, RET_DV))
    kdec = jnp.broadcast_to(jnp.exp((r - 1.0 - tf)[None, :, None] * lg), (RET_HEADS, ROWS, RET_DK))
    cdec = jnp.broadcast_to(jnp.exp(r * lg), (RET_HEADS, RET_DK, RET_DV))
    return decay, qdec, kdec, cdec


def _rotary_tables(pos):
    half = RET_DK // 2
    inv = ROPE_BASE ** (-jnp.arange(half, dtype=F32) / half)
    ang = pos.astype(F32)[:, None] * inv[None, :]
    cos, sin = jnp.cos(ang), jnp.sin(ang)
    return jnp.concatenate([cos, cos], axis=1), jnp.concatenate([-sin, sin], axis=1)


def _ret_body(q_ref, k_ref, v_ref, g_ref, cos_ref, sin_ref, dec_ref, qd_ref, kd_ref, cd_ref, lng_ref,
              s0_ref, *rest, nseq, hps):
    o_ref, s_ref, q_scr, kd_scr, cross_scr = rest[-5:]
    r = ROWS // nseq

    @pl.when(pl.program_id(2) == 0)
    def _():
        s_ref[...] = s0_ref[...]

    cos = cos_ref[...]
    sin = sin_ref[...]

    def rot(x):
        return x * cos + pltpu.roll(x, RET_DK // 2, 1) * sin

    for h in range(hps):
        ks = slice(h * RET_DK, (h + 1) * RET_DK)
        vs = slice(h * RET_DV, (h + 1) * RET_DV)
        q = rot(q_ref[:, ks])
        k = rot(k_ref[:, ks]) * (RET_DK ** -0.5)
        qb = q.astype(BF16)
        vb = v_ref[:, vs].astype(BF16)
        scores = _dot_nt(qb, k.astype(BF16)) * dec_ref[h]
        inner = _dot(scores.astype(BF16), vb)
        kd = k * kd_ref[h]
        cd = cd_ref[h]
        if nseq == 1:
            s = s_ref[0, h]
            cross = _dot(qb, s.astype(BF16))
            s_ref[0, h] = s * cd + _dot_tn(kd.astype(BF16), vb)
        else:
            q_scr[...] = q
            kd_scr[...] = kd

            def seq_step(i, carry, h=h, vs=vs, cd=cd):
                rs = pl.ds(pl.multiple_of(i * r, r), r)
                s = s_ref[i, h]
                cross_scr[rs, :] = _dot(q_scr[rs, :].astype(BF16), s.astype(BF16))
                s_ref[i, h] = s * cd + _dot_tn(kd_scr[rs, :].astype(BF16), v_ref[rs, vs].astype(BF16))
                return carry

            lax.fori_loop(0, nseq, seq_step, 0, unroll=4)
            cross = cross_scr[...]
        o = inner + cross * qd_ref[h]
        mu = jnp.mean(o, axis=-1, keepdims=True)
        oc = o - mu
        var = jnp.mean(oc * oc, axis=-1, keepdims=True)
        on = (oc * lax.rsqrt(var + EPS)) * lng_ref[:, vs]
        o_ref[:, vs] = (_silu(g_ref[:, vs]) * on).astype(o_ref.dtype)


def _stacked_state_io(state_all, lin, out_prev, lout, n_out, blk, idx):
    in_spec = pl.BlockSpec((None,) + blk, lambda *g: (lin,) + idx(*g))
    out_spec = pl.BlockSpec((None,) + blk, lambda *g: (lout,) + idx(*g))
    out_shape = jax.ShapeDtypeStruct((n_out,) + state_all.shape[1:], state_all.dtype)
    assert out_prev.shape == out_shape.shape
    return in_spec, out_spec, out_shape, [pl.BlockSpec(memory_space=pl.ANY)], [out_prev]


def _retention(p_main, nblk, nc, nseq, hps, cos, sin, tabs, ln_g, s0_all, lin, out_prev, lout, n_out):
    h = RET_HEADS
    hb = h // hps
    decay, qdec, kdec, cdec = tabs
    kw, vw = hps * RET_DK, hps * RET_DV
    s_in, s_out, s_shape, x_specs, x_args = _stacked_state_io(
        s0_all, lin, out_prev, lout, n_out, (nseq, hps, RET_DK, RET_DV), lambda b, hh, c: (b, hh, 0, 0))

    in_specs = [
        pl.BlockSpec((ROWS, kw), lambda b, hh, c: (b * nc + c, P_Q // kw + hh)),
        pl.BlockSpec((ROWS, kw), lambda b, hh, c: (b * nc + c, P_K // kw + hh)),
        pl.BlockSpec((ROWS, vw), lambda b, hh, c: (b * nc + c, P_V // vw + hh)),
        pl.BlockSpec((ROWS, vw), lambda b, hh, c: (b * nc + c, P_G // vw + hh)),
        pl.BlockSpec((ROWS, RET_DK), lambda b, hh, c: (c, 0)),
        pl.BlockSpec((ROWS, RET_DK), lambda b, hh, c: (c, 0)),
        pl.BlockSpec((hps, ROWS, ROWS), lambda b, hh, c: (hh, 0, 0)),
        pl.BlockSpec((hps, ROWS, RET_DV), lambda b, hh, c: (hh, 0, 0)),
        pl.BlockSpec((hps, ROWS, RET_DK), lambda b, hh, c: (hh, 0, 0)),
        pl.BlockSpec((hps, RET_DK, RET_DV), lambda b, hh, c: (hh, 0, 0)),
        pl.BlockSpec((1, vw), lambda b, hh, c: (0, hh)),
        s_in,
    ] + x_specs
    out_specs = [pl.BlockSpec((ROWS, vw), lambda b, hh, c: (b * nc + c, hh)), s_out]
    out_shape = [jax.ShapeDtypeStruct((nblk * nc * ROWS, h * RET_DV), BF16), s_shape]
    return pl.pallas_call(
        functools.partial(_ret_body, nseq=nseq, hps=hps),
        grid=(nblk, hb, nc),
        in_specs=in_specs, out_specs=out_specs, out_shape=out_shape,
        scratch_shapes=[pltpu.VMEM((ROWS, RET_DK), F32), pltpu.VMEM((ROWS, RET_DK), F32),
                        pltpu.VMEM((ROWS, RET_DV), F32)],
        input_output_aliases={len(in_specs) - 1: 1} if x_args else {},
        compiler_params=_cparams(("parallel", "parallel", "arbitrary")),
        name="retention",
    )(p_main, p_main, p_main, p_main, cos, sin, decay, qdec, kdec, cdec, ln_g.reshape(1, -1), s0_all, *x_args)


def _s5_disc_body(ar_ref, ai_ref, ldt_ref, br_ref, bi_ref, abr_ref, abi_ref, bbr_ref, bbi_ref):
    ar = ar_ref[...]
    ai = ai_ref[...]
    dt = jnp.exp(ldt_ref[...])
    mag = jnp.exp(dt * ar)
    abr = mag * jnp.cos(dt * ai)
    abi = mag * jnp.sin(dt * ai)
    nr = abr - 1.0
    ni = abi
    den = ar * ar + ai * ai
    f_re = (nr * ar + ni * ai) / den
    f_im = (ni * ar - nr * ai) / den
    br = br_ref[...]
    bi = bi_ref[...]
    abr_ref[...] = abr
    abi_ref[...] = abi
    bbr_ref[...] = f_re * br - f_im * bi
    bbi_ref[...] = f_re * bi + f_im * br


def _s5_discretize(a_re, a_im, log_dt, b_re, b_im):
    g, n, c = b_re.shape
    flat = (g * c * n // 128, 128)

    def rep(a):
        return jnp.broadcast_to(a[:, None, :], (g, c, n)).reshape(flat)

    args = (rep(a_re), rep(a_im), rep(jnp.broadcast_to(log_dt[:, None], (g, n))),
            jnp.swapaxes(b_re, 1, 2).reshape(flat), jnp.swapaxes(b_im, 1, 2).reshape(flat))
    spec = pl.BlockSpec(flat, lambda: (0, 0))
    abr, abi, bbr, bbi = pl.pallas_call(
        _s5_disc_body,
        in_specs=[spec] * 5, out_specs=[spec] * 4,
        out_shape=[jax.ShapeDtypeStruct(flat, F32)] * 4,
        name="s5_discretize",
    )(*args)
    abr = abr.reshape(g, c, n)[:, 0, :].reshape(1, g * n)
    abi = abi.reshape(g, c, n)[:, 0, :].reshape(1, g * n)
    return abr, abi, bbr.reshape(g, c, n), bbi.reshape(g, c, n)


def _s5_block_weights(bb_re, bb_im, c_re, c_im):
    g, c, n = bb_re.shape
    ng = 16
    eye = jnp.eye(ng, dtype=F32)

    def b_blk(bb):
        x = bb.reshape(g // ng, ng, c, n)
        return jnp.einsum("kgcn,gh->kgchn", x, eye).reshape(g // ng, ng * c, ng * n).astype(BF16)

    def c_blk(cc):
        x = cc.reshape(g // ng, ng, c, n)
        return jnp.einsum("kgcn,gh->kgnhc", x, eye).reshape(g // ng, ng * n, ng * c).astype(BF16)

    return b_blk(bb_re), b_blk(bb_im), c_blk(c_re), c_blk(c_im)


def _s5_body(u_ref, bre_ref, bim_ref, cre_ref, cim_ref, ar_ref, ai_ref, d_ref, s0r_ref, s0i_ref,
             y_ref, sr_ref, si_ref, bur, bui, *, tl, bb, wide):
    n_cg = bre_ref.shape[0]
    cw = bre_ref.shape[1]
    sw = bre_ref.shape[2]
    w = n_cg * cw
    lc = 512

    @pl.when(pl.program_id(1) == 0)
    def _():
        sr_ref[...] = s0r_ref[...]
        si_ref[...] = s0i_ref[...]

    if wide:
        ux = u_ref[...]
        u = jnp.swapaxes(jnp.stack([ux[:, b * w:(b + 1) * w] for b in range(bb)], axis=0), 0, 1)
        u = u.reshape(tl * bb, w)
    else:
        u = u_ref[...]
    ub = u.astype(BF16)
    for cg in range(n_cg):
        ucg = ub[:, cg * cw:(cg + 1) * cw]
        bur[:, cg * sw:(cg + 1) * sw] = _dot(ucg, bre_ref[cg])
        bui[:, cg * sw:(cg + 1) * sw] = _dot(ucg, bim_ref[cg])

    for bt in range(bb // 8):
        rs = slice(bt * 8, (bt + 1) * 8)
        for ci in range(n_cg * sw // lc):
            cs = slice(ci * lc, (ci + 1) * lc)
            ar = jnp.broadcast_to(ar_ref[:, cs], (8, lc))
            ai = jnp.broadcast_to(ai_ref[:, cs], (8, lc))

            def step(l, carry, cs=cs, ar=ar, ai=ai, bt=bt):
                xr, xi = carry
                r0 = pl.multiple_of(l * bb + bt * 8, 8)
                nr = (ar * xr - ai * xi) + bur[pl.ds(r0, 8), cs]
                ni = (ar * xi + ai * xr) + bui[pl.ds(r0, 8), cs]
                bur[pl.ds(r0, 8), cs] = nr
                bui[pl.ds(r0, 8), cs] = ni
                return nr, ni

            xr, xi = lax.fori_loop(0, tl, step, (sr_ref[rs, cs], si_ref[rs, cs]), unroll=2)
            sr_ref[rs, cs] = xr
            si_ref[rs, cs] = xi

    for cg in range(n_cg):
        xs = slice(cg * sw, (cg + 1) * sw)
        us = slice(cg * cw, (cg + 1) * cw)
        y = _dot(bur[:, xs].astype(BF16), cre_ref[cg]) - _dot(bui[:, xs].astype(BF16), cim_ref[cg])
        y = jax.nn.gelu(y + d_ref[:, us] * u[:, us])
        if wide:
            y = jnp.swapaxes(y.reshape(tl, bb, cw), 0, 1)
            for b in range(bb):
                y_ref[:, b * w + cg * cw:b * w + (cg + 1) * cw] = y[b].astype(y_ref.dtype)
        else:
            y_ref[:, us] = y.astype(y_ref.dtype)


def _s5(u_tm, wts, abr, abi, d, s0r, s0i, tl, bb):
    b, ns = s0r.shape
    bre, bim, cre, cim = wts
    w = bre.shape[0] * bre.shape[1]
    wide = u_tm.shape[1] != w
    rows = tl * bb
    nbb = b // bb
    if wide:
        assert bb == b and u_tm.shape[1] == b * w
        ntb = u_tm.shape[0] // tl
        io_spec = pl.BlockSpec((tl, b * w), lambda i, j: (j, 0))
    else:
        ntb = u_tm.shape[0] // (rows * nbb)
        io_spec = pl.BlockSpec((rows, w), lambda i, j: (i * ntb + j, 0))
    return pl.pallas_call(
        functools.partial(_s5_body, tl=tl, bb=bb, wide=wide),
        grid=(nbb, ntb),
        in_specs=[
            io_spec,
            _resident(bre.shape), _resident(bim.shape), _resident(cre.shape), _resident(cim.shape),
            _resident((1, ns)), _resident((1, ns)), _resident((1, w)),
            pl.BlockSpec((bb, ns), lambda i, j: (i, 0)),
            pl.BlockSpec((bb, ns), lambda i, j: (i, 0)),
        ],
        out_specs=[
            io_spec,
            pl.BlockSpec((bb, ns), lambda i, j: (i, 0)),
            pl.BlockSpec((bb, ns), lambda i, j: (i, 0)),
        ],
        out_shape=[
            jax.ShapeDtypeStruct(u_tm.shape, BF16),
            jax.ShapeDtypeStruct((b, ns), F32),
            jax.ShapeDtypeStruct((b, ns), F32),
        ],
        scratch_shapes=[pltpu.VMEM((rows, ns), F32), pltpu.VMEM((rows, ns), F32)],
        compiler_params=_cparams(("parallel", "arbitrary")),
        name="s5",
    )(u_tm, bre, bim, cre, cim, abr, abi, d.reshape(1, w), s0r, s0i)


def _to_time_major(u, b, l, bb):
    w = u.shape[-1]
    return u.reshape(b // bb, bb, l, w).transpose(0, 2, 1, 3).reshape(b * l, w)


def _from_time_major(y, b, l, bb):
    w = y.shape[-1]
    return y.reshape(b // bb, l, bb, w).transpose(0, 2, 1, 3).reshape(b * l, w)


def _ssd_body(z_ref, xs_ref, bm_ref, cm_ref, dt_ref, px_ref, pb_ref, pc_ref, wx_ref, wb_ref, wc_ref,
              bx_ref, bb_ref, bc_ref, dtb_ref, alog_ref, dsk_ref, ng_ref, tri_ref, trit_ref, seqm_ref, h0_ref,
              *rest, nseq, gps):
    o_ref, h_ref, tx, tb, tc, xsw_scr, bm_scr, cm_scr, col_scr, yoff_scr = rest[-10:]

    @pl.when(pl.program_id(2) == 0)
    def _():
        h_ref[...] = h0_ref[...]
        if nseq == 1:
            tx[...] = px_ref[0]
            tb[...] = pb_ref[0]
            tc[...] = pc_ref[0]

    for gi in range(gps):
        _ssd_group(gi, z_ref, xs_ref, bm_ref, cm_ref, dt_ref, px_ref, pb_ref, pc_ref, wx_ref, wb_ref, wc_ref,
                   bx_ref, bb_ref, bc_ref, dtb_ref, alog_ref, dsk_ref, ng_ref, tri_ref, trit_ref, seqm_ref,
                   o_ref, h_ref, tx, tb, tc, xsw_scr, bm_scr, cm_scr, col_scr, yoff_scr, nseq)


def _ssd_group(gi, z_ref, xs_ref, bm_ref, cm_ref, dt_ref, px_ref, pb_ref, pc_ref, wx_ref, wb_ref, wc_ref,
               bx_ref, bb_ref, bc_ref, dtb_ref, alog_ref, dsk_ref, ng_ref, tri_ref, trit_ref, seqm_ref,
               o_ref, h_ref, tx, tb, tc, xsw_scr, bm_scr, cm_scr, col_scr, yoff_scr, nseq):
    r = ROWS // nseq
    pd = SSD_HEADDIM
    nh = SSD_HPG
    gw = nh * pd
    gc = slice(gi * gw, (gi + 1) * gw)
    gn = slice(gi * SSD_STATE, (gi + 1) * SSD_STATE)
    gh = slice(gi * nh, (gi + 1) * nh)
    row = lax.broadcasted_iota(jnp.int32, (ROWS, 1), 0)
    t_idx = row & (r - 1)
    lo = lax.broadcasted_iota(jnp.int32, (1, ROWS), 1) < pd

    def conv(x_ref, tail, prev_ref, w_ref, b_ref, cs):
        x = x_ref[:, cs]
        acc = b_ref[:, cs] + w_ref[SSD_CONV - 1:SSD_CONV, cs] * x
        if nseq == 1:
            xe = jnp.concatenate([tail[:, cs], x], axis=0)
            tail[:, cs] = x[ROWS - 8:, :]
        else:
            z8 = prev_ref[:, :, cs].reshape(ROWS, x.shape[1])
        for k in range(1, SSD_CONV):
            if nseq == 1:
                xk = xe[8 - k:8 - k + ROWS, :]
            else:
                xk = jnp.where(t_idx >= k, pltpu.roll(x, k, 0), pltpu.roll(z8, ROWS - 8 + k, 0))
            acc = acc + w_ref[SSD_CONV - 1 - k:SSD_CONV - k, cs] * xk
        return _silu(acc)

    xs = conv(xs_ref, tx, px_ref, wx_ref, bx_ref, gc)
    bm = conv(bm_ref, tb, pb_ref, wb_ref, bb_ref, gn)
    cm = conv(cm_ref, tc, pc_ref, wc_ref, bc_ref, gn)
    bmb = bm.astype(BF16)
    cmb = cm.astype(BF16)
    xsb = xs.astype(BF16)

    dt8 = _softplus(dt_ref[:, gn].T[0:nh, :] + dtb_ref[gh, :])
    dta8 = dt8 * (-jnp.exp(alog_ref[gh, :]))
    cum8 = _dot_f32(dta8, trit_ref[...])
    clast8 = _dot_f32(dta8, seqm_ref[...])
    e8 = jnp.exp(cum8)
    dtw8 = dt8 * jnp.exp(clast8 - cum8)
    ecl8 = jnp.exp(clast8)
    colf = jnp.concatenate([cum8, e8, dtw8, ecl8, jnp.zeros((ROWS - 4 * nh, ROWS), F32)], axis=0).T
    c_cum, c_e, c_dtw, c_ecl = 0, nh, 2 * nh, 3 * nh
    causal = tri_ref[...] > 0.0
    cb = _dot_nt(cmb, bmb)

    def expand(off):
        return jnp.concatenate(
            [jnp.where(lo, colf[:, off + 2 * p:off + 2 * p + 1], colf[:, off + 2 * p + 1:off + 2 * p + 2])
             for p in range(nh // 2)], axis=1)

    ys = []
    zero = jnp.zeros((), BF16)
    for p in range(nh // 2):
        ms = []
        for h in (2 * p, 2 * p + 1):
            seg = colf[:, c_cum + h:c_cum + h + 1] - cum8[h:h + 1, :]
            lm = jnp.where(causal, jnp.exp(seg), 0.0)
            ms.append(((cb * lm) * dt8[h:h + 1, :]).astype(BF16))
        xp = xsb[:, p * 2 * pd:(p + 1) * 2 * pd]
        xcat = jnp.concatenate([jnp.where(lo, xp, zero), jnp.where(lo, zero, xp)], axis=0)
        ys.append(_dot(jnp.concatenate(ms, axis=1), xcat))
    y = jnp.concatenate(ys, axis=1)

    xsw = xs * expand(c_dtw)

    def new_state(hg, col_row, upd):
        parts = [hg[h * pd:(h + 1) * pd, :] * col_row[:, c_ecl + h:c_ecl + h + 1] for h in range(nh)]
        return (jnp.concatenate(parts, axis=0) + upd).reshape(nh, pd, SSD_STATE)

    if nseq == 1:
        hg = h_ref[0, gh].reshape(gw, SSD_STATE)
        yoff = _dot_nt(cmb, hg.astype(BF16))
        h_ref[0, gh] = new_state(hg, colf[0:1, :], _dot_tn(xsw.astype(BF16), bmb))
    else:
        xsw_scr[...] = xsw
        bm_scr[...] = bm
        cm_scr[...] = cm
        col_scr[...] = colf

        def seq_step(i, carry):
            r0 = pl.multiple_of(i * r, r)
            rs = pl.ds(r0, r)
            hg = h_ref[i, gh].reshape(gw, SSD_STATE)
            yoff_scr[rs, :] = _dot_nt(cm_scr[rs, :].astype(BF16), hg.astype(BF16))
            upd = _dot_tn(xsw_scr[rs, :].astype(BF16), bm_scr[rs, :].astype(BF16))
            h_ref[i, gh] = new_state(hg, col_scr[pl.ds(r0, 1), :], upd)
            return carry

        lax.fori_loop(0, nseq, seq_step, 0, unroll=8)
        yoff = yoff_scr[...]

    y = y + yoff * expand(c_e) + dsk_ref[:, gc] * xs
    y = y * _silu(z_ref[:, gc])
    o_ref[:, gc] = (_rms(y) * ng_ref[:, gc]).astype(o_ref.dtype)


def _ssd(p_main, p_dt, nblk, nc, nseq, gps, conv0, conv_w, conv_b, dtb, alog, dskip, norm_g,
         h0_all, lin, out_prev, lout, n_out):
    g = SSD_GROUPS // gps
    gw = gps * SSD_INNER // SSD_GROUPS
    ns = gps * SSD_STATE
    nh = gps * SSD_HPG
    tri, seqm = _block_masks(nseq)
    h_in, h_out, h_shape, x_specs, x_args = _stacked_state_io(
        h0_all, lin, out_prev, lout, n_out, (nseq, nh, SSD_HEADDIM, SSD_STATE), lambda b, gg, c: (b, gg, 0, 0))
    xo = P_XBC
    bo = P_XBC + SSD_INNER
    co = bo + SSD_GROUPS * SSD_STATE

    def cols(width, off):
        return lambda b, gg, c: (0, off // width + gg)

    in_specs = [
        pl.BlockSpec((ROWS, gw), lambda b, gg, c: (b * nc + c, P_Z // gw + gg)),
        pl.BlockSpec((ROWS, gw), lambda b, gg, c: (b * nc + c, xo // gw + gg)),
        pl.BlockSpec((ROWS, ns), lambda b, gg, c: (b * nc + c, bo // ns + gg)),
        pl.BlockSpec((ROWS, ns), lambda b, gg, c: (b * nc + c, co // ns + gg)),
        pl.BlockSpec((ROWS, ns), lambda b, gg, c: (b * nc + c, PG_DT // ns + gg)),
        pl.BlockSpec((nseq, 8, gw), lambda b, gg, c: (b, 0, gg)),
        pl.BlockSpec((nseq, 8, ns), lambda b, gg, c: (b, 0, (bo - xo) // ns + gg)),
        pl.BlockSpec((nseq, 8, ns), lambda b, gg, c: (b, 0, (co - xo) // ns + gg)),
        pl.BlockSpec((SSD_CONV, gw), cols(gw, 0)),
        pl.BlockSpec((SSD_CONV, ns), cols(ns, bo - xo)),
        pl.BlockSpec((SSD_CONV, ns), cols(ns, co - xo)),
        pl.BlockSpec((1, gw), cols(gw, 0)),
        pl.BlockSpec((1, ns), cols(ns, bo - xo)),
        pl.BlockSpec((1, ns), cols(ns, co - xo)),
        pl.BlockSpec((nh, ROWS), lambda b, gg, c: (gg, 0)),
        pl.BlockSpec((nh, ROWS), lambda b, gg, c: (gg, 0)),
        pl.BlockSpec((1, gw), cols(gw, 0)),
        pl.BlockSpec((1, gw), cols(gw, 0)),
        pl.BlockSpec((ROWS, ROWS), lambda b, gg, c: (0, 0)),
        pl.BlockSpec((ROWS, ROWS), lambda b, gg, c: (0, 0)),
        pl.BlockSpec((ROWS, ROWS), lambda b, gg, c: (0, 0)),
        h_in,
    ] + x_specs
    out_specs = [pl.BlockSpec((ROWS, gw), lambda b, gg, c: (b * nc + c, gg)), h_out]
    out_shape = [jax.ShapeDtypeStruct((nblk * nc * ROWS, SSD_INNER), BF16), h_shape]
    assert nseq == 1 or gps == 1
    gw1 = SSD_INNER // SSD_GROUPS
    scratch = [pltpu.VMEM((8, gw), F32), pltpu.VMEM((8, ns), F32), pltpu.VMEM((8, ns), F32),
               pltpu.VMEM((ROWS, gw1), F32), pltpu.VMEM((ROWS, SSD_STATE), F32), pltpu.VMEM((ROWS, SSD_STATE), F32),
               pltpu.VMEM((ROWS, 128), F32), pltpu.VMEM((ROWS, gw1), F32)]
    return pl.pallas_call(
        functools.partial(_ssd_body, nseq=nseq, gps=gps),
        grid=(nblk, g, nc),
        in_specs=in_specs, out_specs=out_specs, out_shape=out_shape,
        scratch_shapes=scratch,
        input_output_aliases={len(in_specs) - 1: 1} if x_args else {},
        compiler_params=_cparams(("parallel", "parallel", "arbitrary")),
        name="ssd",
    )(p_main, p_main, p_main, p_main, p_dt, conv0, conv0, conv0, conv_w, conv_w, conv_w,
      conv_b, conv_b, conv_b, dtb, alog, dskip, norm_g, tri, tri.T, seqm, h0_all, *x_args)


def _head_rows(v):
    return jnp.broadcast_to(v[:, None], (SSD_HEADS, ROWS))


def _gate_dt_weight(w_in_b):
    nl, k, _ = w_in_b.shape
    w_dt = w_in_b[:, :, W_DT:W_GATE].reshape(nl, k, SSD_GROUPS, SSD_HPG)
    w_dt = jnp.pad(w_dt, ((0, 0), (0, 0), (0, 0), (0, 128 - SSD_HPG))).reshape(nl, k, SSD_GROUPS * 128)
    return jnp.concatenate([w_in_b[:, :, W_GATE:], w_dt], axis=2)


def _pad_conv(c):
    return jnp.pad(c, ((0, 0), (8 - (SSD_CONV - 1), 0), (0, 0)))


class _Stream:
    def __init__(self, b, l, pos0, s5_tl, s5_bb, ret_hps, ssd_gps):
        self.b, self.l = b, l
        self.nseq = 1 if l % ROWS == 0 else ROWS // l
        self.nc = max(l // ROWS, 1)
        self.nblk = b // self.nseq
        self.s5_tl, self.s5_bb, self.ret_hps, self.ssd_gps = s5_tl, s5_bb, ret_hps, ssd_gps
        pos = pos0 + jnp.arange(self.nc * ROWS, dtype=jnp.int32) % l
        self.cos, self.sin = _rotary_tables(pos)
        self.rtab = _retention_tables(self.nseq)


def _layer(x, st, w, depth, lout, ret_all, ssm_all, lin, ret_prev, ssm_prev, s5r0, s5i0, conv0):
    b, l = st.b, st.l
    whole_seq = st.nseq == 1
    x, h_mix = _ffn(x, w["ffn1_norm"], w["ffn1_wg"], w["ffn1_wu"], w["ffn1_wd"], w["mix_norm"], BF16)
    p_main = _matmul(h_mix, w["w_in"], layer=lout, tiles=W_MAIN_TILES)
    p_gd = _matmul(h_mix, w["w_gd"], layer=lout, tiles=(0, 1), tile_w=PG_WIDTH // 2)

    o_ret, ret_out = _retention(p_main, st.nblk, st.nc, st.nseq, st.ret_hps, st.cos, st.sin, st.rtab,
                                w["ret_ln_g"], ret_all, lin, ret_prev, lout, depth)

    if whole_seq:
        u_tm = _matmul(h_mix, w["w_in"], seq_len=l, layer=lout, tiles=W_U_TILES)
    else:
        u_tm = _to_time_major(_matmul(h_mix, w["w_in"], layer=lout, tiles=W_U_TILES), b, l, st.s5_bb)
    y5, s5r, s5i = _s5(u_tm, w["s5_wts"], w["s5_abr"], w["s5_abi"], w["s5_d"], s5r0, s5i0, st.s5_tl, st.s5_bb)
    s5_pre = y5 if whole_seq else _from_time_major(y5, b, l, st.s5_bb)

    o_ssd, ssm_out = _ssd(p_main, p_gd, st.nblk, st.nc, st.nseq, st.ssd_gps, conv0, w["conv_w"], w["conv_b"],
                          w["dtb"], w["alog"], w["dskip"], w["ssd_norm"], ssm_all, lin, ssm_prev, lout, depth)
    conv_new = p_main.reshape(b, l, -1)[:, l - (SSD_CONV - 1):, P_XBC:P_XBC + SSD_CONV_DIM]

    x = _merge(x, o_ret, s5_pre, o_ssd, p_gd, w["ret_wo"], w["s5_wglu"], w["ssd_wo"], w["w_out"],
               s5_seq_len=l if whole_seq else None)
    x, y_fin = _ffn(x, w["ffn2_norm"], w["ffn2_wg"], w["ffn2_wu"], w["ffn2_wd"],
                    w["final_norm"] if lout == depth - 1 else None, F32)
    s5_new = jnp.stack([s5r, s5i], axis=-1).reshape(b, S5_GROUPS, S5_STATE, 2)
    return x, y_fin, ret_out, ssm_out, s5_new, conv_new


def kernel(x_prompt, x_sample, state_ret, state_s5, state_ssm, state_conv, ffn1_norm, ffn1_w_gu, ffn1_w_down, mix_norm, w_in, ret_ln_g, ret_w_o, s5_a_re, s5_a_im, s5_log_dt, s5_b_re, s5_b_im, s5_c_re, s5_c_im, s5_d, s5_w_glu, ssd_conv_w, ssd_conv_b, ssd_dt_bias, ssd_a_log, ssd_d, ssd_norm, ssd_w_o, w_out, ffn2_norm, ffn2_w_gu, ffn2_w_down, final_norm):
    depth = w_in.shape[0]
    bp, lp, d = x_prompt.shape
    bs, ls, _ = x_sample.shape
    assert lp % ROWS == 0 and ROWS % ls == 0 and bs % (ROWS // ls) == 0 and bp % 8 == 0 and bs % 32 == 0
    st_p = _Stream(bp, lp, 0, s5_tl=64, s5_bb=bp, ret_hps=RET_HEADS, ssd_gps=SSD_GROUPS)
    st_s = _Stream(bs, ls, PAST_LEN, s5_tl=ls, s5_bb=32, ret_hps=2, ssd_gps=1)

    xp = x_prompt.reshape(bp * lp, d)
    xs = x_sample.reshape(bs * ls, d)
    zero_ret = jnp.zeros((1, bp, RET_HEADS, RET_DK, RET_DV), F32)
    zero_s5 = jnp.zeros((bp, S5_GROUPS * S5_STATE), F32)
    zero_ssm = jnp.zeros((1, bp, SSD_HEADS, SSD_HEADDIM, SSD_STATE), F32)
    zero_conv = jnp.zeros((bp, 8, SSD_CONV_DIM), F32)

    small = [[] for _ in range(4)]
    yp = ys = None
    ret_p = jnp.zeros((depth,) + zero_ret.shape[1:], F32)
    ssm_p = jnp.zeros((depth,) + zero_ssm.shape[1:], F32)
    ret_s = jnp.zeros(state_ret.shape, F32)
    ssm_s = jnp.zeros(state_ssm.shape, F32)
    f = ffn1_w_gu.shape[2] // 2
    w_in_b = w_in.astype(BF16)
    w_gd = _gate_dt_weight(w_in_b)
    for l in range(depth):
        abr, abi, bbr, bbi = _s5_discretize(s5_a_re[l], s5_a_im[l], s5_log_dt[l], s5_b_re[l], s5_b_im[l])
        w = dict(
            ffn1_norm=ffn1_norm[l], ffn1_wg=ffn1_w_gu[l, :, :f].astype(BF16), ffn1_wu=ffn1_w_gu[l, :, f:].astype(BF16),
            ffn1_wd=ffn1_w_down[l].astype(BF16), mix_norm=mix_norm[l], w_in=w_in_b, w_gd=w_gd,
            ret_ln_g=ret_ln_g[l], s5_wts=_s5_block_weights(bbr, bbi, s5_c_re[l], s5_c_im[l]), s5_abr=abr, s5_abi=abi,
            s5_d=s5_d[l], conv_w=ssd_conv_w[l], conv_b=ssd_conv_b[l].reshape(1, -1), dtb=_head_rows(ssd_dt_bias[l]),
            alog=_head_rows(ssd_a_log[l]), dskip=jnp.repeat(ssd_d[l], SSD_HEADDIM).reshape(1, SSD_INNER),
            ssd_norm=ssd_norm[l].reshape(1, SSD_INNER), ret_wo=ret_w_o[l].astype(BF16),
            s5_wglu=s5_w_glu[l].astype(BF16), ssd_wo=ssd_w_o[l].astype(BF16), w_out=w_out[l].astype(BF16),
            ffn2_norm=ffn2_norm[l], ffn2_wg=ffn2_w_gu[l, :, :f].astype(BF16), ffn2_wu=ffn2_w_gu[l, :, f:].astype(BF16),
            ffn2_wd=ffn2_w_down[l].astype(BF16), final_norm=final_norm,
        )
        xp, yp, ret_p, ssm_p, s1, c1 = _layer(xp, st_p, w, depth, l, zero_ret, zero_ssm, 0, ret_p, ssm_p,
                                              zero_s5, zero_s5, zero_conv)
        ss = state_s5[l].reshape(bs, S5_GROUPS * S5_STATE, 2)
        xs, ys, ret_s, ssm_s, s2, c2 = _layer(xs, st_s, w, depth, l, state_ret, state_ssm, l, ret_s, ssm_s,
                                              ss[..., 0], ss[..., 1], _pad_conv(state_conv[l]))
        for lst, v in zip(small, (s1, s2, c1, c2)):
            lst.append(v)

    s5_p, s5_s, conv_p, conv_s = (jnp.stack(o) for o in small)
    return (yp.reshape(bp, lp, d), ys.reshape(bs, ls, d), ret_p, ret_s, s5_p, s5_s, ssm_p, ssm_s, conv_p, conv_s)
```
